```python
import jax, jax.numpy as jnp
from jax import lax
import numpy as np

D_MODEL = 2048
BATCH = 8
SEQ = 2048
DEPTH = 1
DEC_BATCH = 8
DEC_SEQ = 4096
PAST_LEN = 128

GRID_W = 64
ATT_HEADS = 8
ATT_KV_HEADS = 2
HEAD_DIM = 128
ATT_GROUP = ATT_HEADS // ATT_KV_HEADS
ATT_W = ATT_HEADS * HEAD_DIM
ATT_KV_W = ATT_KV_HEADS * HEAD_DIM
AXIS_DIM = HEAD_DIM // 2
ROPE_THETA = 10000.0
Q_BLOCK = 128
GLA_HEADS = 4
GLA_DK = 128
GLA_DV = 256
GLA_QK = GLA_HEADS * GLA_DK
GLA_V = GLA_HEADS * GLA_DV
GLA_RANK = 16
GATE_NORM = 16.0
GLA_CHUNK = 64
N_BRANCH = 2
D_FF = 5632
CONV_W = 3
EPS = 1e-6

IN_SIZES = [ATT_W, ATT_KV_W, ATT_KV_W, GLA_QK, GLA_QK, GLA_V, GLA_V, GLA_RANK, GLA_RANK, D_MODEL, D_MODEL]
IN_COLS = int(sum(IN_SIZES))
IN_SPLITS = [int(s) for s in np.cumsum(IN_SIZES)[:-1]]

kernel_name = "hybrid_gqa_gla_convffn_adaln_encoder"


def _rmsnorm(x, g):
    xf = x.astype(jnp.float32)
    y = xf * lax.rsqrt(jnp.mean(xf * xf, axis=-1, keepdims=True) + EPS)
    return (y * g.astype(jnp.float32)).astype(x.dtype)


def _rope_tables(T):
    rows = T // GRID_W
    row = jnp.repeat(jnp.arange(rows, dtype=jnp.float32), GRID_W)
    col = jnp.tile(jnp.arange(GRID_W, dtype=jnp.float32), rows)
    inv = ROPE_THETA ** (-jnp.arange(0, AXIS_DIM, 2, dtype=jnp.float32) / AXIS_DIM)
    ar = row[:, None] * inv
    ac = col[:, None] * inv
    ang = jnp.concatenate([ar, ar, ac, ac], axis=-1)
    return jnp.cos(ang), jnp.sin(ang)


def _apply_rope(x, cos, sin):
    xf = x.astype(jnp.float32)
    xr = xf.reshape(*x.shape[:-1], 2, 2, AXIS_DIM // 2)
    rot = jnp.stack([-xr[..., 1, :], xr[..., 0, :]], axis=-2).reshape(x.shape)
    return (xf * cos[:, None, :] + rot * sin[:, None, :]).astype(x.dtype)


def _attention(q, k, v):
    B, T = q.shape[0], q.shape[1]
    nb = T // Q_BLOCK
    qb = q.reshape(B, nb, Q_BLOCK, ATT_KV_HEADS, ATT_GROUP, HEAD_DIM).transpose(1, 0, 3, 4, 2, 5)
    kt = k.transpose(0, 2, 1, 3)
    vt = v.transpose(0, 2, 1, 3)
    scale = HEAD_DIM ** -0.5

    def blk(qi):
        s = jnp.einsum('bkgqd,bksd->bkgqs', qi, kt).astype(jnp.float32) * scale
        p = jax.nn.softmax(s, axis=-1).astype(vt.dtype)
        return jnp.einsum('bkgqs,bksd->bkgqd', p, vt)

    o = lax.map(blk, qb)
    return o.transpose(1, 0, 4, 2, 3, 5).reshape(B, T, ATT_W)


def _gla_direction(q, k, v, log_a):
    B, T, H, DK = q.shape
    DV = v.shape[-1]
    nc = T // GLA_CHUNK

    def chunks(a):
        return a.reshape(B, nc, GLA_CHUNK, H, a.shape[-1]).transpose(1, 0, 3, 2, 4)

    qc, kc, vc, gc = chunks(q), chunks(k), chunks(v), chunks(log_a)
    bc = jnp.cumsum(gc, axis=-2)
    mask = jnp.tril(jnp.ones((GLA_CHUNK, GLA_CHUNK), dtype=bool))

    def step(S, inp):
        qi, ki, vi, bi = inp
        q_e = qi * jnp.exp(bi)
        k_e = ki * jnp.exp(-bi)
        A = jnp.where(mask, jnp.einsum('bhtd,bhsd->bhts', q_e, k_e), 0.0)
        o = jnp.einsum('bhts,bhsv->bhtv', A, vi) + jnp.einsum('bhtd,bhdv->bhtv', q_e, S)
        b_last = bi[:, :, -1:, :]
        S = jnp.exp(b_last[:, :, 0, :, None]) * S + jnp.einsum('bhsd,bhsv->bhdv', ki * jnp.exp(b_last - bi), vi)
        return S, o

    S0 = jnp.zeros((B, H, DK, DV), jnp.float32)
    _, o = lax.scan(step, S0, (qc, kc, vc, bc))
    return o.transpose(1, 0, 3, 2, 4).reshape(B, T, H, DV)


def _gla_mixer(q, k, v, r, low_f, low_b, w_a_up_f, b_a_f, w_a_up_b, b_a_b, g_gla):
    B, T = q.shape[0], q.shape[1]
    f32 = jnp.float32
    qh = q.astype(f32).reshape(B, T, GLA_HEADS, GLA_DK) * (GLA_DK ** -0.5)
    kh = k.astype(f32).reshape(B, T, GLA_HEADS, GLA_DK)
    vh = v.astype(f32).reshape(B, T, GLA_HEADS, GLA_DV)
    la_f = (jax.nn.log_sigmoid((low_f @ w_a_up_f + b_a_f).astype(f32)) / GATE_NORM).reshape(B, T, GLA_HEADS, GLA_DK)
    la_b = (jax.nn.log_sigmoid((low_b @ w_a_up_b + b_a_b).astype(f32)) / GATE_NORM).reshape(B, T, GLA_HEADS, GLA_DK)
    o_f = _gla_direction(qh, kh, vh, la_f)
    o_b = _gla_direction(qh[:, ::-1], kh[:, ::-1], vh[:, ::-1], la_b[:, ::-1])[:, ::-1]
    diag = jnp.sum(qh * kh, axis=-1, keepdims=True) * vh
    o = _rmsnorm(o_f + o_b - diag, g_gla.reshape(GLA_HEADS, GLA_DV))
    return o.reshape(B, T, GLA_V).astype(r.dtype) * jax.nn.silu(r)


def _layer(x, c, w_mod, b_mod, g_mix_norm, w_in, g_q, g_k, w_a_up_f, b_a_f, w_a_up_b, b_a_b,
           g_gla, w_br_att, w_br_gla, w_out, g_ffn_norm, w_up, w_conv, b_conv, w_down):
    B, T, _ = x.shape
    mod = (jax.nn.silu(c) @ w_mod + b_mod)[:, None, :]
    sh1, sc1, gt1, sh2, sc2, gt2 = jnp.split(mod, 6, axis=-1)

    h = _rmsnorm(x, g_mix_norm) * (1 + sc1) + sh1
    proj = h @ w_in
    (aq, ak, av, gq, gk, gv, gr, low_f, low_b, gate_a, gate_g) = jnp.split(proj, IN_SPLITS, axis=-1)

    cos, sin = _rope_tables(T)
    qh = _apply_rope(_rmsnorm(aq.reshape(B, T, ATT_HEADS, HEAD_DIM), g_q), cos, sin)
    kh = _apply_rope(_rmsnorm(ak.reshape(B, T, ATT_KV_HEADS, HEAD_DIM), g_k), cos, sin)
    vh = av.reshape(B, T, ATT_KV_HEADS, HEAD_DIM)
    att = _attention(qh, kh, vh)

    gla = _gla_mixer(gq, gk, gv, gr, low_f, low_b, w_a_up_f, b_a_f, w_a_up_b, b_a_b, g_gla)

    merged = jax.nn.sigmoid(gate_a) * (att @ w_br_att) + jax.nn.sigmoid(gate_g) * (gla @ w_br_gla)
    x = x + gt1 * (merged @ w_out)

    h2 = _rmsnorm(x, g_ffn_norm) * (1 + sc2) + sh2
    u = h2 @ w_up
    up = jnp.pad(u, ((0, 0), (1, 1), (0, 0)))
    u = w_conv[0] * up[:, :-2] + w_conv[1] * up[:, 1:-1] + w_conv[2] * up[:, 2:] + b_conv
    val, gate = jnp.split(u, 2, axis=-1)
    x = x + gt2 * ((jax.nn.silu(gate) * val) @ w_down)
    return x


def _trunk(x, c, w_mod, b_mod, g_mix_norm, w_in, g_q, g_k, w_a_up_f, b_a_f, w_a_up_b, b_a_b,
           g_gla, w_br_att, w_br_gla, w_out, g_ffn_norm, w_up, w_conv, b_conv, w_down, g_final):
    for l in range(DEPTH):
        x = _layer(x, c, w_mod[l], b_mod[l], g_mix_norm[l], w_in[l], g_q[l], g_k[l],
                   w_a_up_f[l], b_a_f[l], w_a_up_b[l], b_a_b[l], g_gla[l], w_br_att[l],
                   w_br_gla[l], w_out[l], g_ffn_norm[l], w_up[l], w_conv[l], b_conv[l], w_down[l])
    return _rmsnorm(x, g_final)


def setup_inputs(seed: int = 0) -> dict:
    key = jax.random.key(seed)
    ks = jax.random.split(key, 32)
    f32 = jnp.float32
    L, D = DEPTH, D_MODEL

    def nrm(k, shape, scale):
        return jax.random.normal(k, shape, f32) * scale

    def gain(k, shape):
        return 1.0 + 0.02 * jax.random.normal(k, shape, f32)

    return {
        "x_prompt": nrm(ks[0], (BATCH, SEQ, D), 1.0),
        "x_sample": nrm(ks[1], (DEC_BATCH, DEC_SEQ, D), 1.0),
        "c_prompt": nrm(ks[2], (BATCH, D), 1.0),
        "c_sample": nrm(ks[3], (DEC_BATCH, D), 1.0),
        "w_mod": nrm(ks[4], (L, D, 6 * D), 0.5 * D ** -0.5),
        "b_mod": nrm(ks[5], (L, 6 * D), 0.01),
        "g_mix_norm": gain(ks[6], (L, D)),
        "w_in": nrm(ks[7], (L, D, IN_COLS), D ** -0.5),
        "g_q": gain(ks[8], (L, HEAD_DIM)),
        "g_k": gain(ks[9], (L, HEAD_DIM)),
        "w_a_up_f": nrm(ks[10], (L, GLA_RANK, GLA_QK), GLA_RANK ** -0.5),
        "b_a_f": nrm(ks[11], (L, GLA_QK), 0.1),
        "w_a_up_b": nrm(ks[12], (L, GLA_RANK, GLA_QK), GLA_RANK ** -0.5),
        "b_a_b": nrm(ks[13], (L, GLA_QK), 0.1),
        "g_gla": gain(ks[14], (L, GLA_V)),
        "w_br_att": nrm(ks[15], (L, ATT_W, D), ATT_W ** -0.5),
        "w_br_gla": nrm(ks[16], (L, GLA_V, D), GLA_V ** -0.5),
        "w_out": nrm(ks[17], (L, D, D), D ** -0.5),
        "g_ffn_norm": gain(ks[18], (L, D)),
        "w_up": nrm(ks[19], (L, D, 2 * D_FF), D ** -0.5),
        "w_conv": nrm(ks[20], (L, CONV_W, 2 * D_FF), CONV_W ** -0.5),
        "b_conv": nrm(ks[21], (L, 2 * D_FF), 0.01),
        "w_down": nrm(ks[22], (L, D_FF, D), D_FF ** -0.5),
        "g_final": gain(ks[23], (D,)),
    }


def reference(x_prompt, x_sample, c_prompt, c_sample, w_mod, b_mod, g_mix_norm, w_in, g_q, g_k,
              w_a_up_f, b_a_f, w_a_up_b, b_a_b, g_gla, w_br_att, w_br_gla, w_out, g_ffn_norm,
              w_up, w_conv, b_conv, w_down, g_final):
    y_prompt = _trunk(x_prompt, c_prompt, w_mod, b_mod, g_mix_norm, w_in, g_q, g_k, w_a_up_f, b_a_f,
                      w_a_up_b, b_a_b, g_gla, w_br_att, w_br_gla, w_out, g_ffn_norm, w_up, w_conv,
                      b_conv, w_down, g_final)
    y_sample = _trunk(x_sample, c_sample, w_mod, b_mod, g_mix_norm, w_in, g_q, g_k, w_a_up_f, b_a_f,
                      w_a_up_b, b_a_b, g_gla, w_br_att, w_br_gla, w_out, g_ffn_norm, w_up, w_conv,
                      b_conv, w_down, g_final)
    return (y_prompt, y_sample)
```

```python
import functools

import jax
import jax.numpy as jnp
from jax import lax
from jax.experimental import pallas as pl
from jax.experimental.pallas import tpu as pltpu

F32 = jnp.float32
BF16 = jnp.bfloat16
EPS = 1e-6

HEAD_DIM = 128
ATT_HEADS = 8
ATT_KV_HEADS = 2
ATT_GROUP = ATT_HEADS // ATT_KV_HEADS
ATT_W = ATT_HEADS * HEAD_DIM
ATT_KV_W = ATT_KV_HEADS * HEAD_DIM
GRID_W = 64
ROPE_THETA = 10000.0
GLA_HEADS = 4
GLA_DK = 128
GLA_DV = 256
GLA_QK = GLA_HEADS * GLA_DK
GLA_V = GLA_HEADS * GLA_DV
GLA_RANK = 16
GATE_NORM = 16.0
GLA_CHUNK = 64
LANES = 128
BF16_ROWS = 16
VMEM_LIMIT = 56 * 1024 * 1024

IN_TN = 512
ATT_COLS = ATT_W + 2 * ATT_KV_W
GLA_COLS = 2 * GLA_QK + 2 * GLA_V
ATT_TILES = ATT_COLS // IN_TN
GLA_TILES = GLA_COLS // IN_TN

TILES = dict(inproj=512, attn=256, gla_rows=256, merge=256, ffn=512, ffn_cols=512)

NT = (((1,), (1,)), ((), ()))
TN = (((0,), (0,)), ((), ()))


def _params(sem):
    return pltpu.CompilerParams(dimension_semantics=sem, vmem_limit_bytes=VMEM_LIMIT)


def _silu(v):
    return v * jax.nn.sigmoid(v)


def _mod_kernel(c_ref, w_ref, b_ref, o_ref):
    s = _silu(c_ref[...]).astype(BF16)
    o_ref[...] = jnp.dot(s, w_ref[...], preferred_element_type=F32) + b_ref[...]


def _mod(c, w_mod, b_mod):
    n, d = c.shape
    cols = w_mod.shape[1]
    tn = 2048
    return pl.pallas_call(
        _mod_kernel,
        grid=(cols // tn,),
        in_specs=[pl.BlockSpec((n, d), lambda j: (0, 0)),
                  pl.BlockSpec((d, tn), lambda j: (0, j)),
                  pl.BlockSpec((1, tn), lambda j: (0, j))],
        out_specs=pl.BlockSpec((n, tn), lambda j: (0, j)),
        out_shape=jax.ShapeDtypeStruct((n, cols), F32),
        compiler_params=_params(("arbitrary",)),
        name="mod",
    )(c, w_mod, b_mod.reshape(1, cols))


def _norm_rope(xh, g, cos, sin_lo, sin_hi, scale):
    ms = jnp.mean(xh * xh, axis=-1, keepdims=True)
    y = xh * lax.rsqrt(ms + EPS) * g
    from_below = pltpu.roll(y, 32, axis=1)
    from_above = pltpu.roll(y, HEAD_DIM - 32, axis=1)
    return (y * cos + from_above * sin_lo + from_below * sin_hi) * scale


def _inproj_kernel(x_ref, sc_ref, sh_ref, g_ref, w_ref, wlow_ref, gq_ref, gk_ref, cos_ref, slo_ref, shi_ref,
                   att_ref, gla_ref, gate_ref, low_ref, h_scr):
    n = pl.program_id(2)

    @pl.when(n == 0)
    def _():
        x = x_ref[...]
        ms = jnp.mean(x * x, axis=-1, keepdims=True)
        h = x * lax.rsqrt(ms + EPS) * g_ref[...]
        hb = (h * (1.0 + sc_ref[...]) + sh_ref[...]).astype(BF16)
        h_scr[...] = hb
        low_ref[...] = jnp.dot(hb, wlow_ref[...], preferred_element_type=F32)

    acc = jnp.dot(h_scr[...], w_ref[...], preferred_element_type=F32)
    cos, slo, shi = cos_ref[...], slo_ref[...], shi_ref[...]

    @pl.when(n < ATT_W // IN_TN)
    def _():
        for hh in range(IN_TN // HEAD_DIM):
            cs = slice(hh * HEAD_DIM, (hh + 1) * HEAD_DIM)
            att_ref[:, cs] = _norm_rope(acc[:, cs], gq_ref[...], cos, slo, shi, HEAD_DIM ** -0.5).astype(BF16)

    @pl.when(n == ATT_W // IN_TN)
    def _():
        for hh in range(ATT_KV_HEADS):
            cs = slice(hh * HEAD_DIM, (hh + 1) * HEAD_DIM)
            att_ref[:, cs] = _norm_rope(acc[:, cs], gk_ref[...], cos, slo, shi, 1.0).astype(BF16)
        att_ref[:, ATT_KV_W:] = acc[:, ATT_KV_W:].astype(BF16)

    @pl.when(n == ATT_TILES)
    def _():
        gla_ref[...] = acc * (GLA_DK ** -0.5)

    @pl.when((n > ATT_TILES) & (n < ATT_TILES + GLA_TILES))
    def _():
        gla_ref[...] = acc

    @pl.when(n >= ATT_TILES + GLA_TILES)
    def _():
        gate_ref[...] = acc


def _inproj(x, mod6, g_mix, w_main, w_low, g_q, g_k, cos, sin_lo, sin_hi):
    B, T, D = x.shape
    tm = min(TILES["inproj"], T)
    n_tiles = w_main.shape[1] // IN_TN
    gate_tiles = n_tiles - ATT_TILES - GLA_TILES
    row = lambda b, i, n: (b, i, 0)
    vec = lambda b, i, n: (0, 0)
    tab = lambda b, i, n: (i, 0)
    return pl.pallas_call(
        _inproj_kernel,
        grid=(B, T // tm, n_tiles),
        in_specs=[
            pl.BlockSpec((None, tm, D), row),
            pl.BlockSpec((None, 1, D), lambda b, i, n: (b * 6 + 1, 0, 0)),
            pl.BlockSpec((None, 1, D), lambda b, i, n: (b * 6 + 0, 0, 0)),
            pl.BlockSpec((1, D), vec),
            pl.BlockSpec((D, IN_TN), lambda b, i, n: (0, n)),
            pl.BlockSpec((D, LANES), vec),
            pl.BlockSpec((1, HEAD_DIM), vec),
            pl.BlockSpec((1, HEAD_DIM), vec),
            pl.BlockSpec((tm, HEAD_DIM), tab),
            pl.BlockSpec((tm, HEAD_DIM), tab),
            pl.BlockSpec((tm, HEAD_DIM), tab),
        ],
        out_specs=[
            pl.BlockSpec((None, tm, IN_TN), lambda b, i, n: (b, i, jnp.minimum(n, ATT_TILES - 1))),
            pl.BlockSpec((None, tm, IN_TN), lambda b, i, n: (b, i, jnp.clip(n - ATT_TILES, 0, GLA_TILES - 1))),
            pl.BlockSpec((None, tm, IN_TN),
                         lambda b, i, n: (b, i, jnp.clip(n - ATT_TILES - GLA_TILES, 0, gate_tiles - 1))),
            pl.BlockSpec((None, tm, LANES), row),
        ],
        out_shape=[
            jax.ShapeDtypeStruct((B, T, ATT_COLS), BF16),
            jax.ShapeDtypeStruct((B, T, GLA_COLS), F32),
            jax.ShapeDtypeStruct((B, T, gate_tiles * IN_TN), F32),
            jax.ShapeDtypeStruct((B, T, LANES), F32),
        ],
        scratch_shapes=[pltpu.VMEM((tm, D), BF16)],
        compiler_params=_params(("arbitrary", "arbitrary", "arbitrary")),
        name="inproj",
    )(x, mod6, mod6, g_mix, w_main, w_low, g_q, g_k, cos, sin_lo, sin_hi)


def _attn_kernel(q_ref, k_ref, v_ref, o_ref):
    k = k_ref[...]
    v = v_ref[...]
    for g in range(ATT_GROUP):
        cs = slice(g * HEAD_DIM, (g + 1) * HEAD_DIM)
        s = lax.dot_general(q_ref[:, cs], k, NT, preferred_element_type=F32)
        m = jnp.max(s, axis=-1, keepdims=True)
        p = jnp.exp(s - m)
        l = jnp.sum(p, axis=-1, keepdims=True)
        o = jnp.dot(p.astype(BF16), v, preferred_element_type=F32)
        o_ref[:, cs] = (o / l).astype(BF16)


def _attention(qkv):
    B, T, _ = qkv.shape
    tq = min(TILES["attn"], T)
    gw = ATT_GROUP * HEAD_DIM
    k0 = ATT_W // HEAD_DIM
    v0 = (ATT_W + ATT_KV_W) // HEAD_DIM
    return pl.pallas_call(
        _attn_kernel,
        grid=(B, ATT_KV_HEADS, T // tq),
        in_specs=[
            pl.BlockSpec((None, tq, gw), lambda b, kv, i: (b, i, kv)),
            pl.BlockSpec((None, T, HEAD_DIM), lambda b, kv, i: (b, 0, k0 + kv)),
            pl.BlockSpec((None, T, HEAD_DIM), lambda b, kv, i: (b, 0, v0 + kv)),
        ],
        out_specs=pl.BlockSpec((None, tq, gw), lambda b, kv, i: (b, i, kv)),
        out_shape=jax.ShapeDtypeStruct((B, T, ATT_W), BF16),
        compiler_params=_params(("arbitrary", "arbitrary", "arbitrary")),
        name="attn",
    )(qkv, qkv, qkv)


def _log_sigmoid(z):
    return jnp.minimum(z, 0.0) - jnp.log(1.0 + jnp.exp(-jnp.abs(z)))


def _gla_chunk(q_ref, k_ref, v_ref, low_ref, w_ref, b_ref, o_ref, s_scr, d, r0, forward):
    C = GLA_CHUNK
    rows = slice(r0, r0 + C)
    t_i = lax.broadcasted_iota(jnp.int32, (C, C), 0)
    s_i = lax.broadcasted_iota(jnp.int32, (C, C), 1)
    if forward:
        cum = (s_i <= t_i)
        keep = cum
    else:
        cum = (s_i >= t_i)
        keep = (s_i > t_i)
    cum = jnp.where(cum, 1.0, 0.0).astype(BF16)

    z = jnp.dot(low_ref[rows, :].astype(BF16), w_ref[...], preferred_element_type=F32) + b_ref[...]
    la = _log_sigmoid(z) * (1.0 / GATE_NORM)
    hi = la.astype(BF16)
    r1 = la - hi.astype(F32)
    mid = r1.astype(BF16)
    lo = (r1 - mid.astype(F32)).astype(BF16)
    bc = (jnp.dot(cum, hi, preferred_element_type=F32) + jnp.dot(cum, mid, preferred_element_type=F32)
          + jnp.dot(cum, lo, preferred_element_type=F32))
    btot = bc[C - 1:C, :] if forward else bc[0:1, :]
    e_pos = jnp.exp(bc)
    e_neg = jnp.exp(-bc)
    e_dec = jnp.exp(btot - bc)
    e_tot = jnp.exp(btot)

    for h in range(GLA_HEADS):
        ks = slice(h * GLA_DK, (h + 1) * GLA_DK)
        vs = slice(h * GLA_DV, (h + 1) * GLA_DV)
        q = q_ref[rows, ks]
        k = k_ref[rows, ks]
        v = v_ref[rows, vs].astype(BF16)
        qe = (q * e_pos[:, ks]).astype(BF16)
        ke = (k * e_neg[:, ks]).astype(BF16)
        kd = (k * e_dec[:, ks]).astype(BF16)
        a = lax.dot_general(qe, ke, NT, preferred_element_type=F32)
        a = jnp.where(keep, a, 0.0).astype(BF16)
        st = s_scr[d, h]
        o = (jnp.dot(a, v, preferred_element_type=F32)
             + lax.dot_general(qe, st.astype(BF16), NT, preferred_element_type=F32))
        o_ref[rows, vs] = o
        s_scr[d, h] = st * e_tot[:, ks] + lax.dot_general(v, kd, TN, preferred_element_type=F32)


def _gla_kernel(qf, kf, vf, lf, qb, kb, vb, lb, wf, wb, bf_, bb_, of, ob, s_scr, *, sub):
    @pl.when(pl.program_id(1) == 0)
    def _():
        s_scr[...] = jnp.zeros_like(s_scr)

    for j in range(sub):
        _gla_chunk(qf, kf, vf, lf, wf, bf_, of, s_scr, 0, j * GLA_CHUNK, True)
    for j in reversed(range(sub)):
        _gla_chunk(qb, kb, vb, lb, wb, bb_, ob, s_scr, 1, j * GLA_CHUNK, False)


def _gla(gla_in, low, w_up_f, b_f, w_up_b, b_b):
    B, T, _ = gla_in.shape
    rb = min(TILES["gla_rows"], T)
    nb = T // rb
    fwd = lambda b, c: (b, c, 0)
    bwd = lambda b, c: (b, nb - 1 - c, 0)
    vec = lambda b, c: (0, 0)

    def streams(row_map):
        r = lambda col: (lambda b, c: (*row_map(b, c)[:2], col))
        return [pl.BlockSpec((None, rb, GLA_QK), r(0)),
                pl.BlockSpec((None, rb, GLA_QK), r(1)),
                pl.BlockSpec((None, rb, GLA_V), r(1)),
                pl.BlockSpec((None, rb, LANES), r(0))]

    return pl.pallas_call(
        functools.partial(_gla_kernel, sub=rb // GLA_CHUNK),
        grid=(B, nb),
        in_specs=streams(fwd) + streams(bwd) + [
            pl.BlockSpec((LANES, GLA_QK), vec), pl.BlockSpec((LANES, GLA_QK), vec),
            pl.BlockSpec((1, GLA_QK), vec), pl.BlockSpec((1, GLA_QK), vec)],
        out_specs=[pl.BlockSpec((None, rb, GLA_V), fwd), pl.BlockSpec((None, rb, GLA_V), bwd)],
        out_shape=[jax.ShapeDtypeStruct((B, T, GLA_V), F32)] * 2,
        scratch_shapes=[pltpu.VMEM((2, GLA_HEADS, GLA_DV, GLA_DK), F32)],
        compiler_params=_params(("arbitrary", "arbitrary")),
        name="gla",
    )(gla_in, gla_in, gla_in, low, gla_in, gla_in, gla_in, low, w_up_f, w_up_b, b_f, b_b)


def _merge_kernel(att_ref, of_ref, ob_ref, r_ref, ga_ref, gg_ref, x_ref, gt_ref, sc_ref, sh_ref, ggla_ref, gffn_ref,
                  wa_ref, wg_ref, wo_ref, x1_ref, h2_ref):
    parts = []
    for h in range(GLA_HEADS):
        vs = slice(h * GLA_DV, (h + 1) * GLA_DV)
        o = of_ref[:, vs] + ob_ref[:, vs]
        ms = jnp.mean(o * o, axis=-1, keepdims=True)
        y = o * lax.rsqrt(ms + EPS) * ggla_ref[:, vs]
        parts.append((y * _silu(r_ref[:, vs])).astype(BF16))
    gl = jnp.concatenate(parts, axis=-1)
    a = jnp.dot(att_ref[...], wa_ref[...], preferred_element_type=F32)
    g = jnp.dot(gl, wg_ref[...], preferred_element_type=F32)
    merged = (jax.nn.sigmoid(ga_ref[...]) * a + jax.nn.sigmoid(gg_ref[...]) * g).astype(BF16)
    out = jnp.dot(merged, wo_ref[...], preferred_element_type=F32)
    x1 = x_ref[...] + gt_ref[...] * out
    x1_ref[...] = x1
    ms = jnp.mean(x1 * x1, axis=-1, keepdims=True)
    h2 = x1 * lax.rsqrt(ms + EPS) * gffn_ref[...]
    h2_ref[...] = (h2 * (1.0 + sc_ref[...]) + sh_ref[...]).astype(BF16)


def _merge(att, o_f, o_b, gla_in, gates, x, mod6, g_gla, g_ffn, w_br_att, w_br_gla, w_out):
    B, T, D = x.shape
    tm = min(TILES["merge"], T)
    row = lambda b, i: (b, i, 0)
    vec = lambda b, i: (0, 0)
    resident = lambda shape: pl.BlockSpec(shape, vec, pipeline_mode=pl.Buffered(1))
    return pl.pallas_call(
        _merge_kernel,
        grid=(B, T // tm),
        in_specs=[
            pl.BlockSpec((None, tm, ATT_W), row),
            pl.BlockSpec((None, tm, GLA_V), row),
            pl.BlockSpec((None, tm, GLA_V), row),
            pl.BlockSpec((None, tm, GLA_V), lambda b, i: (b, i, GLA_COLS // GLA_V - 1)),
            pl.BlockSpec((None, tm, D), lambda b, i: (b, i, 0)),
            pl.BlockSpec((None, tm, D), lambda b, i: (b, i, 1)),
            pl.BlockSpec((None, tm, D), row),
            pl.BlockSpec((None, 1, D), lambda b, i: (b * 6 + 2, 0, 0)),
            pl.BlockSpec((None, 1, D), lambda b, i: (b * 6 + 4, 0, 0)),
            pl.BlockSpec((None, 1, D), lambda b, i: (b * 6 + 3, 0, 0)),
            pl.BlockSpec((1, GLA_V), vec),
            pl.BlockSpec((1, D), vec),
            resident((ATT_W, D)),
            resident((GLA_V, D)),
            resident((D, D)),
        ],
        out_specs=[pl.BlockSpec((None, tm, D), row), pl.BlockSpec((None, tm, D), row)],
        out_shape=[jax.ShapeDtypeStruct((B, T, D), F32), jax.ShapeDtypeStruct((B, T, D), BF16)],
        compiler_params=_params(("arbitrary", "arbitrary")),
        name="merge",
    )(att, o_f, o_b, gla_in, gates, gates, x, mod6, mod6, mod6, g_gla, g_ffn, w_br_att, w_br_gla, w_out)


def _ffn_kernel(h_ref, hp_ref, hn_ref, wv_ref, wg_ref, cv_ref, cg_ref, bv_ref, bg_ref, wd_ref, x1_ref, gt_ref,
                gf_ref, y_ref, hext, acc, *, tm):
    i = pl.program_id(1)
    j = pl.program_id(2)
    halo = BF16_ROWS

    @pl.when(j == 0)
    def _():
        has_prev = (i > 0).astype(F32)
        has_next = (i < pl.num_programs(1) - 1).astype(F32)
        hext[0:halo, :] = (hp_ref[...].astype(F32) * has_prev).astype(BF16)
        hext[halo:halo + tm, :] = h_ref[...]
        hext[halo + tm:, :] = (hn_ref[...].astype(F32) * has_next).astype(BF16)
        acc[...] = jnp.zeros_like(acc)

    he = hext[...]
    ext = tm + 2 * halo

    def conv(u, c_ref, b_ref):
        prev = pltpu.roll(u, 1, axis=0)[halo:halo + tm]
        nxt = pltpu.roll(u, ext - 1, axis=0)[halo:halo + tm]
        return c_ref[0:1, :] * prev + c_ref[1:2, :] * u[halo:halo + tm] + c_ref[2:3, :] * nxt + b_ref[...]

    val = conv(jnp.dot(he, wv_ref[...], preferred_element_type=F32), cv_ref, bv_ref)
    gate = conv(jnp.dot(he, wg_ref[...], preferred_element_type=F32), cg_ref, bg_ref)
    act = (_silu(gate) * val).astype(BF16)
    acc[...] += jnp.dot(act, wd_ref[...], preferred_element_type=F32)

    @pl.when(j == pl.num_programs(2) - 1)
    def _():
        x2 = x1_ref[...] + gt_ref[...] * acc[...]
        ms = jnp.mean(x2 * x2, axis=-1, keepdims=True)
        y_ref[...] = x2 * lax.rsqrt(ms + EPS) * gf_ref[...]


def _ffn(h2, x1, mod6, w_up, w_conv, b_conv, w_down, g_final):
    B, T, D = x1.shape
    d_ff = w_down.shape[0]
    tm = min(TILES["ffn"], T)
    tf = TILES["ffn_cols"]
    nj = d_ff // tf
    halo = BF16_ROWS
    row = lambda b, i, j: (b, i, 0)
    return pl.pallas_call(
        functools.partial(_ffn_kernel, tm=tm),
        grid=(B, T // tm, nj),
        in_specs=[
            pl.BlockSpec((None, tm, D), row),
            pl.BlockSpec((None, halo, D), lambda b, i, j: (b, jnp.maximum(i * (tm // halo) - 1, 0), 0)),
            pl.BlockSpec((None, halo, D), lambda b, i, j: (b, jnp.minimum((i + 1) * (tm // halo), T // halo - 1), 0)),
            pl.BlockSpec((D, tf), lambda b, i, j: (0, j)),
            pl.BlockSpec((D, tf), lambda b, i, j: (0, nj + j)),
            pl.BlockSpec((3, tf), lambda b, i, j: (0, j)),
            pl.BlockSpec((3, tf), lambda b, i, j: (0, nj + j)),
            pl.BlockSpec((1, tf), lambda b, i, j: (0, j)),
            pl.BlockSpec((1, tf), lambda b, i, j: (0, nj + j)),
            pl.BlockSpec((tf, D), lambda b, i, j: (j, 0)),
            pl.BlockSpec((None, tm, D), row),
            pl.BlockSpec((None, 1, D), lambda b, i, j: (b * 6 + 5, 0, 0)),
            pl.BlockSpec((1, D), lambda b, i, j: (0, 0)),
        ],
        out_specs=pl.BlockSpec((None, tm, D), row),
        out_shape=jax.ShapeDtypeStruct((B, T, D), F32),
        scratch_shapes=[pltpu.VMEM((tm + 2 * halo, D), BF16), pltpu.VMEM((tm, D), F32)],
        compiler_params=_params(("arbitrary", "arbitrary", "arbitrary")),
        name="ffn",
    )(h2, h2, h2, w_up, w_up, w_conv, w_conv, b_conv, b_conv, w_down, x1, mod6, g_final)


def _rope_tables(T):
    rows = T // GRID_W
    axis_dim = HEAD_DIM // 2
    row = jnp.repeat(jnp.arange(rows, dtype=F32), GRID_W)
    col = jnp.tile(jnp.arange(GRID_W, dtype=F32), rows)
    inv = ROPE_THETA ** (-jnp.arange(0, axis_dim, 2, dtype=F32) / axis_dim)
    ar = row[:, None] * inv
    ac = col[:, None] * inv
    ang = jnp.concatenate([ar, ar, ac, ac], axis=-1)
    cos, sin = jnp.cos(ang), jnp.sin(ang)
    lower = (jnp.arange(HEAD_DIM) % axis_dim) < axis_dim // 2
    return cos, jnp.where(lower, -sin, 0.0), jnp.where(lower, 0.0, sin)


def _prep_weights(w_mod, b_mod, g_mix_norm, w_in, g_q, g_k, w_a_up_f, b_a_f, w_a_up_b, b_a_b, g_gla, w_br_att,
                  w_br_gla, w_out, g_ffn_norm, w_up, w_conv, b_conv, w_down, g_final):
    D = w_in.shape[1]
    low0 = ATT_COLS + GLA_COLS
    low1 = low0 + 2 * GLA_RANK
    wi = w_in[0]
    w_main = jnp.concatenate([wi[:, :low0], wi[:, low1:]], axis=1).astype(BF16)
    w_low = jnp.pad(wi[:, low0:low1], ((0, 0), (0, LANES - 2 * GLA_RANK))).astype(BF16)
    up_f = jnp.pad(w_a_up_f[0], ((0, LANES - GLA_RANK), (0, 0))).astype(BF16)
    up_b = jnp.pad(w_a_up_b[0], ((GLA_RANK, LANES - 2 * GLA_RANK), (0, 0))).astype(BF16)
    return dict(
        w_mod=w_mod[0].astype(BF16), b_mod=b_mod[0], g_mix=g_mix_norm[0].reshape(1, D),
        w_main=w_main, w_low=w_low, g_q=g_q[0].reshape(1, HEAD_DIM), g_k=g_k[0].reshape(1, HEAD_DIM),
        up_f=up_f, up_b=up_b, b_f=b_a_f[0].reshape(1, GLA_QK), b_b=b_a_b[0].reshape(1, GLA_QK),
        g_gla=g_gla[0].reshape(1, GLA_V), w_br_att=w_br_att[0].astype(BF16), w_br_gla=w_br_gla[0].astype(BF16),
        w_out=w_out[0].astype(BF16), g_ffn=g_ffn_norm[0].reshape(1, D), w_up=w_up[0].astype(BF16),
        w_conv=w_conv[0], b_conv=b_conv[0].reshape(1, -1), w_down=w_down[0].astype(BF16),
        g_final=g_final.reshape(1, D))


def _trunk(x, mod, W):
    B, T, D = x.shape
    mod6 = mod.reshape(B * 6, 1, D)
    cos, sin_lo, sin_hi = _rope_tables(T)
    qkv, gla_in, gates, low = _inproj(x, mod6, W["g_mix"], W["w_main"], W["w_low"], W["g_q"], W["g_k"],
                                      cos, sin_lo, sin_hi)
    att = _attention(qkv)
    o_f, o_b = _gla(gla_in, low, W["up_f"], W["b_f"], W["up_b"], W["b_b"])
    x1, h2 = _merge(att, o_f, o_b, gla_in, gates, x, mod6, W["g_gla"], W["g_ffn"], W["w_br_att"], W["w_br_gla"],
                    W["w_out"])
    return _ffn(h2, x1, mod6, W["w_up"], W["w_conv"], W["b_conv"], W["w_down"], W["g_final"])


def kernel(x_prompt, x_sample, c_prompt, c_sample, w_mod, b_mod, g_mix_norm, w_in, g_q, g_k, w_a_up_f, b_a_f,
           w_a_up_b, b_a_b, g_gla, w_br_att, w_br_gla, w_out, g_ffn_norm, w_up, w_conv, b_conv, w_down, g_final):
    assert w_mod.shape[0] == 1, "single-layer trunk"
    W = _prep_weights(w_mod, b_mod, g_mix_norm, w_in, g_q, g_k, w_a_up_f, b_a_f, w_a_up_b, b_a_b, g_gla, w_br_att,
                      w_br_gla, w_out, g_ffn_norm, w_up, w_conv, b_conv, w_down, g_final)
    nb = x_prompt.shape[0]
    mod = _mod(jnp.concatenate([c_prompt, c_sample], axis=0), W["w_mod"], W["b_mod"])
    return _trunk(x_prompt, mod[:nb], W), _trunk(x_sample, mod[nb:], W)
```

```python
import functools

import jax
import jax.numpy as jnp
from jax import lax
from jax.experimental import pallas as pl
from jax.experimental.pallas import tpu as pltpu

F32 = jnp.float32
BF16 = jnp.bfloat16
EPS = 1e-6

HEAD_DIM = 128
ATT_HEADS = 8
ATT_KV_HEADS = 2
ATT_GROUP = ATT_HEADS // ATT_KV_HEADS
ATT_W = ATT_HEADS * HEAD_DIM
ATT_KV_W = ATT_KV_HEADS * HEAD_DIM
GRID_W = 64
ROPE_THETA = 10000.0
GLA_HEADS = 4
GLA_DK = 128
GLA_DV = 256
GLA_QK = GLA_HEADS * GLA_DK
GLA_V = GLA_HEADS * GLA_DV
GLA_RANK = 16
GATE_NORM = 16.0
GLA_CHUNK = 64
LANES = 128
BF16_ROWS = 16
VMEM_LIMIT = 56 * 1024 * 1024

IN_TN = 512
ATT_COLS = ATT_W + 2 * ATT_KV_W
QK_COLS = 2 * GLA_QK
VR_COLS = 2 * GLA_V
ATT_TILES = ATT_COLS // IN_TN
QK_TILES = QK_COLS // IN_TN
VR_TILES = VR_COLS // IN_TN
QK_END = ATT_TILES + QK_TILES
VR_END = QK_END + VR_TILES

TILES = dict(inproj=1024, attn=256, gla_rows=256, merge=256, ffn=512, ffn_cols=512)

NT = (((1,), (1,)), ((), ()))
TN = (((0,), (0,)), ((), ()))


def _params(sem):
    return pltpu.CompilerParams(dimension_semantics=sem, vmem_limit_bytes=VMEM_LIMIT)


def _silu(v):
    return v * jax.nn.sigmoid(v)


def _mod_kernel(c_ref, w_ref, b_ref, o_ref):
    s = _silu(c_ref[...]).astype(BF16)
    o_ref[...] = jnp.dot(s, w_ref[...], preferred_element_type=F32) + b_ref[...]


def _mod(c, w_mod, b_mod):
    n, d = c.shape
    cols = w_mod.shape[1]
    tn = 2048
    return pl.pallas_call(
        _mod_kernel,
        grid=(cols // tn,),
        in_specs=[pl.BlockSpec((n, d), lambda j: (0, 0)),
                  pl.BlockSpec((d, tn), lambda j: (0, j)),
                  pl.BlockSpec((1, tn), lambda j: (0, j))],
        out_specs=pl.BlockSpec((n, tn), lambda j: (0, j)),
        out_shape=jax.ShapeDtypeStruct((n, cols), F32),
        compiler_params=_params(("arbitrary",)),
        name="mod",
    )(c, w_mod, b_mod.reshape(1, cols))


def _norm_rope(xh, g, cos, sin_lo, sin_hi, scale):
    ms = jnp.mean(xh * xh, axis=-1, keepdims=True)
    y = xh * lax.rsqrt(ms + EPS) * g
    from_below = pltpu.roll(y, 32, axis=1)
    from_above = pltpu.roll(y, HEAD_DIM - 32, axis=1)
    return (y * cos + from_above * sin_lo + from_below * sin_hi) * scale


def _inproj_kernel(x_ref, sc_ref, sh_ref, g_ref, w_ref, wlow_ref, gq_ref, gk_ref, cos_ref, slo_ref, shi_ref,
                   att_ref, qk_ref, vr_ref, gate_ref, low_ref, h_scr):
    n = pl.program_id(2)

    @pl.when(n == 0)
    def _():
        x = x_ref[...]
        ms = jnp.mean(x * x, axis=-1, keepdims=True)
        h = x * lax.rsqrt(ms + EPS) * g_ref[...]
        hb = (h * (1.0 + sc_ref[...]) + sh_ref[...]).astype(BF16)
        h_scr[...] = hb
        low_ref[...] = jnp.dot(hb, wlow_ref[...], preferred_element_type=F32)

    def proj():
        return jnp.dot(h_scr[...], w_ref[...], preferred_element_type=F32)

    @pl.when(n < ATT_W // IN_TN)
    def _():
        acc = proj()
        cos, slo, shi = cos_ref[...], slo_ref[...], shi_ref[...]
        for hh in range(IN_TN // HEAD_DIM):
            cs = slice(hh * HEAD_DIM, (hh + 1) * HEAD_DIM)
            att_ref[:, cs] = _norm_rope(acc[:, cs], gq_ref[...], cos, slo, shi, HEAD_DIM ** -0.5).astype(BF16)

    @pl.when(n == ATT_W // IN_TN)
    def _():
        acc = proj()
        cos, slo, shi = cos_ref[...], slo_ref[...], shi_ref[...]
        for hh in range(ATT_KV_HEADS):
            cs = slice(hh * HEAD_DIM, (hh + 1) * HEAD_DIM)
            att_ref[:, cs] = _norm_rope(acc[:, cs], gk_ref[...], cos, slo, shi, 1.0).astype(BF16)
        att_ref[:, ATT_KV_W:] = acc[:, ATT_KV_W:].astype(BF16)

    @pl.when(n == ATT_TILES)
    def _():
        qk_ref[...] = proj() * (GLA_DK ** -0.5)

    @pl.when((n > ATT_TILES) & (n < QK_END))
    def _():
        qk_ref[...] = proj()

    @pl.when((n >= QK_END) & (n < VR_END))
    def _():
        vr_ref[...] = proj().astype(BF16)

    @pl.when(n >= VR_END)
    def _():
        gate_ref[...] = proj().astype(BF16)


def _inproj(x, mod6, g_mix, w_main, w_low, g_q, g_k, cos, sin_lo, sin_hi):
    B, T, D = x.shape
    tm = min(TILES["inproj"], T)
    n_tiles = w_main.shape[1] // IN_TN
    gate_tiles = n_tiles - VR_END
    row = lambda b, i, n: (b, i, 0)
    vec = lambda b, i, n: (0, 0)
    tab = lambda b, i, n: (i, 0)
    return pl.pallas_call(
        _inproj_kernel,
        grid=(B, T // tm, n_tiles),
        in_specs=[
            pl.BlockSpec((None, tm, D), row),
            pl.BlockSpec((None, 1, D), lambda b, i, n: (b * 6 + 1, 0, 0)),
            pl.BlockSpec((None, 1, D), lambda b, i, n: (b * 6 + 0, 0, 0)),
            pl.BlockSpec((1, D), vec),
            pl.BlockSpec((D, IN_TN), lambda b, i, n: (0, n)),
            pl.BlockSpec((D, LANES), vec),
            pl.BlockSpec((1, HEAD_DIM), vec),
            pl.BlockSpec((1, HEAD_DIM), vec),
            pl.BlockSpec((tm, HEAD_DIM), tab),
            pl.BlockSpec((tm, HEAD_DIM), tab),
            pl.BlockSpec((tm, HEAD_DIM), tab),
        ],
        out_specs=[
            pl.BlockSpec((None, tm, IN_TN), lambda b, i, n: (b, i, jnp.minimum(n, ATT_TILES - 1))),
            pl.BlockSpec((None, tm, IN_TN), lambda b, i, n: (b, i, jnp.clip(n - ATT_TILES, 0, QK_TILES - 1))),
            pl.BlockSpec((None, tm, IN_TN), lambda b, i, n: (b, i, jnp.clip(n - QK_END, 0, VR_TILES - 1))),
            pl.BlockSpec((None, tm, IN_TN), lambda b, i, n: (b, i, jnp.clip(n - VR_END, 0, gate_tiles - 1))),
            pl.BlockSpec((None, tm, LANES), row),
        ],
        out_shape=[
            jax.ShapeDtypeStruct((B, T, ATT_COLS), BF16),
            jax.ShapeDtypeStruct((B, T, QK_COLS), F32),
            jax.ShapeDtypeStruct((B, T, VR_COLS), BF16),
            jax.ShapeDtypeStruct((B, T, gate_tiles * IN_TN), BF16),
            jax.ShapeDtypeStruct((B, T, LANES), F32),
        ],
        scratch_shapes=[pltpu.VMEM((tm, D), BF16)],
        compiler_params=_params(("arbitrary", "arbitrary", "arbitrary")),
        name="inproj",
    )(x, mod6, mod6, g_mix, w_main, w_low, g_q, g_k, cos, sin_lo, sin_hi)


def _attn_kernel(q_ref, k_ref, v_ref, o_ref):
    k = k_ref[...]
    v = v_ref[...]
    for g in range(ATT_GROUP):
        cs = slice(g * HEAD_DIM, (g + 1) * HEAD_DIM)
        s = lax.dot_general(q_ref[:, cs], k, NT, preferred_element_type=F32)
        m = jnp.max(s, axis=-1, keepdims=True)
        p = jnp.exp(s - m)
        l = jnp.sum(p, axis=-1, keepdims=True)
        o = jnp.dot(p.astype(BF16), v, preferred_element_type=F32)
        o_ref[:, cs] = (o / l).astype(BF16)


def _attention(qkv):
    B, T, _ = qkv.shape
    tq = min(TILES["attn"], T)
    gw = ATT_GROUP * HEAD_DIM
    k0 = ATT_W // HEAD_DIM
    v0 = (ATT_W + ATT_KV_W) // HEAD_DIM
    return pl.pallas_call(
        _attn_kernel,
        grid=(B, ATT_KV_HEADS, T // tq),
        in_specs=[
            pl.BlockSpec((None, tq, gw), lambda b, kv, i: (b, i, kv)),
            pl.BlockSpec((None, T, HEAD_DIM), lambda b, kv, i: (b, 0, k0 + kv)),
            pl.BlockSpec((None, T, HEAD_DIM), lambda b, kv, i: (b, 0, v0 + kv)),
        ],
        out_specs=pl.BlockSpec((None, tq, gw), lambda b, kv, i: (b, i, kv)),
        out_shape=jax.ShapeDtypeStruct((B, T, ATT_W), BF16),
        compiler_params=_params(("arbitrary", "arbitrary", "arbitrary")),
        name="attn",
    )(qkv, qkv, qkv)


def _log_sigmoid(z):
    return jnp.minimum(z, 0.0) - jnp.log(1.0 + jnp.exp(-jnp.abs(z)))


def _gla_chunk(q_ref, k_ref, v_ref, low_ref, w_ref, b_ref, o_ref, s_scr, d, r0, forward):
    C = GLA_CHUNK
    rows = slice(r0, r0 + C)
    t_i = lax.broadcasted_iota(jnp.int32, (C, C), 0)
    s_i = lax.broadcasted_iota(jnp.int32, (C, C), 1)
    if forward:
        cum = (s_i <= t_i)
        keep = cum
    else:
        cum = (s_i >= t_i)
        keep = (s_i > t_i)
    cum = jnp.where(cum, 1.0, 0.0).astype(BF16)

    z = jnp.dot(low_ref[rows, :].astype(BF16), w_ref[...], preferred_element_type=F32) + b_ref[...]
    la = _log_sigmoid(z) * (1.0 / GATE_NORM)
    hi = la.astype(BF16)
    r1 = la - hi.astype(F32)
    mid = r1.astype(BF16)
    lo = (r1 - mid.astype(F32)).astype(BF16)
    bc = (jnp.dot(cum, hi, preferred_element_type=F32) + jnp.dot(cum, mid, preferred_element_type=F32)
          + jnp.dot(cum, lo, preferred_element_type=F32))
    btot = bc[C - 1:C, :] if forward else bc[0:1, :]
    e_pos = jnp.exp(bc)
    e_neg = jnp.exp(-bc)
    e_dec = jnp.exp(btot - bc)
    e_tot = jnp.exp(btot)

    for h in range(GLA_HEADS):
        ks = slice(h * GLA_DK, (h + 1) * GLA_DK)
        vs = slice(h * GLA_DV, (h + 1) * GLA_DV)
        q = q_ref[rows, ks]
        k = k_ref[rows, ks]
        v = v_ref[rows, vs]
        qe = (q * e_pos[:, ks]).astype(BF16)
        ke = (k * e_neg[:, ks]).astype(BF16)
        kd = (k * e_dec[:, ks]).astype(BF16)
        a = lax.dot_general(qe, ke, NT, preferred_element_type=F32)
        a = jnp.where(keep, a, 0.0).astype(BF16)
        st = s_scr[d, h]
        o = (jnp.dot(a, v, preferred_element_type=F32)
             + lax.dot_general(qe, st.astype(BF16), NT, preferred_element_type=F32))
        o_ref[rows, vs] = o
        s_scr[d, h] = st * e_tot[:, ks] + lax.dot_general(v, kd, TN, preferred_element_type=F32)


def _gla_kernel(qf, kf, vf, lf, qb, kb, vb, lb, wf, wb, bf_, bb_, of, ob, s_scr, *, sub):
    @pl.when(pl.program_id(1) == 0)
    def _():
        s_scr[...] = jnp.zeros_like(s_scr)

    for j in range(sub):
        _gla_chunk(qf, kf, vf, lf, wf, bf_, of, s_scr, 0, j * GLA_CHUNK, True)
    for j in reversed(range(sub)):
        _gla_chunk(qb, kb, vb, lb, wb, bb_, ob, s_scr, 1, j * GLA_CHUNK, False)


def _gla(qk, vr, low, w_up_f, b_f, w_up_b, b_b):
    B, T, _ = qk.shape
    rb = min(TILES["gla_rows"], T)
    nb = T // rb
    fwd = lambda b, c: (b, c, 0)
    bwd = lambda b, c: (b, nb - 1 - c, 0)
    vec = lambda b, c: (0, 0)

    def streams(row_map):
        r = lambda col: (lambda b, c: (*row_map(b, c)[:2], col))
        return [pl.BlockSpec((None, rb, GLA_QK), r(0)),
                pl.BlockSpec((None, rb, GLA_QK), r(1)),
                pl.BlockSpec((None, rb, GLA_V), r(0)),
                pl.BlockSpec((None, rb, LANES), r(0))]

    return pl.pallas_call(
        functools.partial(_gla_kernel, sub=rb // GLA_CHUNK),
        grid=(B, nb),
        in_specs=streams(fwd) + streams(bwd) + [
            pl.BlockSpec((LANES, GLA_QK), vec), pl.BlockSpec((LANES, GLA_QK), vec),
            pl.BlockSpec((1, GLA_QK), vec), pl.BlockSpec((1, GLA_QK), vec)],
        out_specs=[pl.BlockSpec((None, rb, GLA_V), fwd), pl.BlockSpec((None, rb, GLA_V), bwd)],
        out_shape=[jax.ShapeDtypeStruct((B, T, GLA_V), F32)] * 2,
        scratch_shapes=[pltpu.VMEM((2, GLA_HEADS, GLA_DV, GLA_DK), F32)],
        compiler_params=_params(("arbitrary", "arbitrary")),
        name="gla",
    )(qk, qk, vr, low, qk, qk, vr, low, w_up_f, w_up_b, b_f, b_b)


def _merge_kernel(att_ref, of_ref, ob_ref, r_ref, ga_ref, gg_ref, x_ref, gt_ref, sc_ref, sh_ref, ggla_ref, gffn_ref,
                  wa_ref, wg_ref, wo_ref, x1_ref, h2_ref):
    parts = []
    for h in range(GLA_HEADS):
        vs = slice(h * GLA_DV, (h + 1) * GLA_DV)
        o = of_ref[:, vs] + ob_ref[:, vs]
        ms = jnp.mean(o * o, axis=-1, keepdims=True)
        y = o * lax.rsqrt(ms + EPS) * ggla_ref[:, vs]
        parts.append((y * _silu(r_ref[:, vs].astype(F32))).astype(BF16))
    gl = jnp.concatenate(parts, axis=-1)
    a = jnp.dot(att_ref[...], wa_ref[...], preferred_element_type=F32)
    g = jnp.dot(gl, wg_ref[...], preferred_element_type=F32)
    gate_a = jax.nn.sigmoid(ga_ref[...].astype(F32))
    gate_g = jax.nn.sigmoid(gg_ref[...].astype(F32))
    merged = (gate_a * a + gate_g * g).astype(BF16)
    out = jnp.dot(merged, wo_ref[...], preferred_element_type=F32)
    x1 = x_ref[...] + gt_ref[...] * out
    x1_ref[...] = x1
    ms = jnp.mean(x1 * x1, axis=-1, keepdims=True)
    h2 = x1 * lax.rsqrt(ms + EPS) * gffn_ref[...]
    h2_ref[...] = (h2 * (1.0 + sc_ref[...]) + sh_ref[...]).astype(BF16)


def _merge(att, o_f, o_b, vr, gates, x, mod6, g_gla, g_ffn, w_br_att, w_br_gla, w_out):
    B, T, D = x.shape
    tm = min(TILES["merge"], T)
    row = lambda b, i: (b, i, 0)
    vec = lambda b, i: (0, 0)
    resident = lambda shape: pl.BlockSpec(shape, vec, pipeline_mode=pl.Buffered(1))
    return pl.pallas_call(
        _merge_kernel,
        grid=(B, T // tm),
        in_specs=[
            pl.BlockSpec((None, tm, ATT_W), row),
            pl.BlockSpec((None, tm, GLA_V), row),
            pl.BlockSpec((None, tm, GLA_V), row),
            pl.BlockSpec((None, tm, GLA_V), lambda b, i: (b, i, 1)),
            pl.BlockSpec((None, tm, D), lambda b, i: (b, i, 0)),
            pl.BlockSpec((None, tm, D), lambda b, i: (b, i, 1)),
            pl.BlockSpec((None, tm, D), row),
            pl.BlockSpec((None, 1, D), lambda b, i: (b * 6 + 2, 0, 0)),
            pl.BlockSpec((None, 1, D), lambda b, i: (b * 6 + 4, 0, 0)),
            pl.BlockSpec((None, 1, D), lambda b, i: (b * 6 + 3, 0, 0)),
            pl.BlockSpec((1, GLA_V), vec),
            pl.BlockSpec((1, D), vec),
            resident((ATT_W, D)),
            resident((GLA_V, D)),
            resident((D, D)),
        ],
        out_specs=[pl.BlockSpec((None, tm, D), row), pl.BlockSpec((None, tm, D), row)],
        out_shape=[jax.ShapeDtypeStruct((B, T, D), F32), jax.ShapeDtypeStruct((B, T, D), BF16)],
        compiler_params=_params(("arbitrary", "arbitrary")),
        name="merge",
    )(att, o_f, o_b, vr, gates, gates, x, mod6, mod6, mod6, g_gla, g_ffn, w_br_att, w_br_gla, w_out)


def _ffn_kernel(h_ref, hp_ref, hn_ref, wv_ref, wg_ref, cv_ref, cg_ref, bv_ref, bg_ref, wd_ref, x1_ref, gt_ref,
                gf_ref, y_ref, hext, acc, *, tm):
    i = pl.program_id(1)
    j = pl.program_id(2)
    halo = BF16_ROWS

    @pl.when(j == 0)
    def _():
        has_prev = (i > 0).astype(F32)
        has_next = (i < pl.num_programs(1) - 1).astype(F32)
        hext[0:halo, :] = (hp_ref[...].astype(F32) * has_prev).astype(BF16)
        hext[halo:halo + tm, :] = h_ref[...]
        hext[halo + tm:, :] = (hn_ref[...].astype(F32) * has_next).astype(BF16)
        acc[...] = jnp.zeros_like(acc)

    he = hext[...]
    ext = tm + 2 * halo

    def conv(u, c_ref, b_ref):
        prev = pltpu.roll(u, 1, axis=0)[halo:halo + tm]
        nxt = pltpu.roll(u, ext - 1, axis=0)[halo:halo + tm]
        return c_ref[0:1, :] * prev + c_ref[1:2, :] * u[halo:halo + tm] + c_ref[2:3, :] * nxt + b_ref[...]

    val = conv(jnp.dot(he, wv_ref[...], preferred_element_type=F32), cv_ref, bv_ref)
    gate = conv(jnp.dot(he, wg_ref[...], preferred_element_type=F32), cg_ref, bg_ref)
    act = (_silu(gate) * val).astype(BF16)
    acc[...] += jnp.dot(act, wd_ref[...], preferred_element_type=F32)

    @pl.when(j == pl.num_programs(2) - 1)
    def _():
        x2 = x1_ref[...] + gt_ref[...] * acc[...]
        ms = jnp.mean(x2 * x2, axis=-1, keepdims=True)
        y_ref[...] = x2 * lax.rsqrt(ms + EPS) * gf_ref[...]


def _ffn(h2, x1, mod6, w_up, w_conv, b_conv, w_down, g_final):
    B, T, D = x1.shape
    d_ff = w_down.shape[0]
    tm = min(TILES["ffn"], T)
    tf = TILES["ffn_cols"]
    nj = d_ff // tf
    halo = BF16_ROWS
    row = lambda b, i, j: (b, i, 0)
    return pl.pallas_call(
        functools.partial(_ffn_kernel, tm=tm),
        grid=(B, T // tm, nj),
        in_specs=[
            pl.BlockSpec((None, tm, D), row),
            pl.BlockSpec((None, halo, D), lambda b, i, j: (b, jnp.maximum(i * (tm // halo) - 1, 0), 0)),
            pl.BlockSpec((None, halo, D), lambda b, i, j: (b, jnp.minimum((i + 1) * (tm // halo), T // halo - 1), 0)),
            pl.BlockSpec((D, tf), lambda b, i, j: (0, j)),
            pl.BlockSpec((D, tf), lambda b, i, j: (0, nj + j)),
            pl.BlockSpec((3, tf), lambda b, i, j: (0, j)),
            pl.BlockSpec((3, tf), lambda b, i, j: (0, nj + j)),
            pl.BlockSpec((1, tf), lambda b, i, j: (0, j)),
            pl.BlockSpec((1, tf), lambda b, i, j: (0, nj + j)),
            pl.BlockSpec((tf, D), lambda b, i, j: (j, 0)),
            pl.BlockSpec((None, tm, D), row),
            pl.BlockSpec((None, 1, D), lambda b, i, j: (b * 6 + 5, 0, 0)),
            pl.BlockSpec((1, D), lambda b, i, j: (0, 0)),
        ],
        out_specs=pl.BlockSpec((None, tm, D), row),
        out_shape=jax.ShapeDtypeStruct((B, T, D), F32),
        scratch_shapes=[pltpu.VMEM((tm + 2 * halo, D), BF16), pltpu.VMEM((tm, D), F32)],
        compiler_params=_params(("arbitrary", "arbitrary", "arbitrary")),
        name="ffn",
    )(h2, h2, h2, w_up, w_up, w_conv, w_conv, b_conv, b_conv, w_down, x1, mod6, g_final)


def _rope_tables(T):
    rows = T // GRID_W
    axis_dim = HEAD_DIM // 2
    row = jnp.repeat(jnp.arange(rows, dtype=F32), GRID_W)
    col = jnp.tile(jnp.arange(GRID_W, dtype=F32), rows)
    inv = ROPE_THETA ** (-jnp.arange(0, axis_dim, 2, dtype=F32) / axis_dim)
    ar = row[:, None] * inv
    ac = col[:, None] * inv
    ang = jnp.concatenate([ar, ar, ac, ac], axis=-1)
    cos, sin = jnp.cos(ang), jnp.sin(ang)
    lower = (jnp.arange(HEAD_DIM) % axis_dim) < axis_dim // 2
    return cos, jnp.where(lower, -sin, 0.0), jnp.where(lower, 0.0, sin)


def _prep_weights(w_mod, b_mod, g_mix_norm, w_in, g_q, g_k, w_a_up_f, b_a_f, w_a_up_b, b_a_b, g_gla, w_br_att,
                  w_br_gla, w_out, g_ffn_norm, w_up, w_conv, b_conv, w_down, g_final):
    D = w_in.shape[1]
    low0 = ATT_COLS + QK_COLS + VR_COLS
    low1 = low0 + 2 * GLA_RANK
    wi = w_in[0]
    w_main = jnp.concatenate([wi[:, :low0], wi[:, low1:]], axis=1).astype(BF16)
    w_low = jnp.pad(wi[:, low0:low1], ((0, 0), (0, LANES - 2 * GLA_RANK))).astype(BF16)
    up_f = jnp.pad(w_a_up_f[0], ((0, LANES - GLA_RANK), (0, 0))).astype(BF16)
    up_b = jnp.pad(w_a_up_b[0], ((GLA_RANK, LANES - 2 * GLA_RANK), (0, 0))).astype(BF16)
    return dict(
        w_mod=w_mod[0].astype(BF16), b_mod=b_mod[0], g_mix=g_mix_norm[0].reshape(1, D),
        w_main=w_main, w_low=w_low, g_q=g_q[0].reshape(1, HEAD_DIM), g_k=g_k[0].reshape(1, HEAD_DIM),
        up_f=up_f, up_b=up_b, b_f=b_a_f[0].reshape(1, GLA_QK), b_b=b_a_b[0].reshape(1, GLA_QK),
        g_gla=g_gla[0].reshape(1, GLA_V), w_br_att=w_br_att[0].astype(BF16), w_br_gla=w_br_gla[0].astype(BF16),
        w_out=w_out[0].astype(BF16), g_ffn=g_ffn_norm[0].reshape(1, D), w_up=w_up[0].astype(BF16),
        w_conv=w_conv[0], b_conv=b_conv[0].reshape(1, -1), w_down=w_down[0].astype(BF16),
        g_final=g_final.reshape(1, D))


def _trunk(x, mod, W):
    B, T, D = x.shape
    mod6 = mod.reshape(B * 6, 1, D)
    cos, sin_lo, sin_hi = _rope_tables(T)
    qkv, qk, vr, gates, low = _inproj(x, mod6, W["g_mix"], W["w_main"], W["w_low"], W["g_q"], W["g_k"],
                                      cos, sin_lo, sin_hi)
    att = _attention(qkv)
    o_f, o_b = _gla(qk, vr, low, W["up_f"], W["b_f"], W["up_b"], W["b_b"])
    x1, h2 = _merge(att, o_f, o_b, vr, gates, x, mod6, W["g_gla"], W["g_ffn"], W["w_br_att"], W["w_br_gla"],
                    W["w_out"])
    return _ffn(h2, x1, mod6, W["w_up"], W["w_conv"], W["b_conv"], W["w_down"], W["g_final"])


def kernel(x_prompt, x_sample, c_prompt, c_sample, w_mod, b_mod, g_mix_norm, w_in, g_q, g_k, w_a_up_f, b_a_f,
           w_a_up_b, b_a_b, g_gla, w_br_att, w_br_gla, w_out, g_ffn_norm, w_up, w_conv, b_conv, w_down, g_final):
    assert w_mod.shape[0] == 1, "single-layer trunk"
    W = _prep_weights(w_mod, b_mod, g_mix_norm, w_in, g_q, g_k, w_a_up_f, b_a_f, w_a_up_b, b_a_b, g_gla, w_br_att,
                      w_br_gla, w_out, g_ffn_norm, w_up, w_conv, b_conv, w_down, g_final)
    nb = x_prompt.shape[0]
    mod = _mod(jnp.concatenate([c_prompt, c_sample], axis=0), W["w_mod"], W["b_mod"])
    return _trunk(x_prompt, mod[:nb], W), _trunk(x_sample, mod[nb:], W)
```

```python
import functools

import jax
import jax.numpy as jnp
from jax import lax
from jax.experimental import pallas as pl
from jax.experimental.pallas import tpu as pltpu

F32 = jnp.float32
BF16 = jnp.bfloat16
EPS = 1e-6
LOG2_E = 1.4426950408889634

HEAD_DIM = 128
ATT_HEADS = 8
ATT_KV_HEADS = 2
ATT_GROUP = ATT_HEADS // ATT_KV_HEADS
ATT_W = ATT_HEADS * HEAD_DIM
ATT_KV_W = ATT_KV_HEADS * HEAD_DIM
GRID_W = 64
ROPE_THETA = 10000.0
GLA_HEADS = 4
GLA_DK = 128
GLA_DV = 256
GLA_QK = GLA_HEADS * GLA_DK
GLA_V = GLA_HEADS * GLA_DV
GLA_RANK = 16
GATE_NORM = 16.0
GLA_CHUNK = 64
LANES = 128
BF16_ROWS = 16
VMEM_LIMIT = 56 * 1024 * 1024

IN_TN = 512
ATT_COLS = ATT_W + 2 * ATT_KV_W
QK_COLS = 2 * GLA_QK
VR_COLS = 2 * GLA_V
ATT_TILES = ATT_COLS // IN_TN
QK_TILES = QK_COLS // IN_TN
VR_TILES = VR_COLS // IN_TN
QK_END = ATT_TILES + QK_TILES
VR_END = QK_END + VR_TILES

TILES = dict(inproj=1024, attn=512, gla_rows=256, merge=256, ffn=512, ffn_cols=512)
ATT_UNIT_ROWS = 128

NT = (((1,), (1,)), ((), ()))
TN = (((0,), (0,)), ((), ()))


def _params(sem):
    return pltpu.CompilerParams(dimension_semantics=sem, vmem_limit_bytes=VMEM_LIMIT)


def _silu(v):
    return v * jax.nn.sigmoid(v)


def _mod_kernel(c_ref, w_ref, b_ref, o_ref):
    s = _silu(c_ref[...]).astype(BF16)
    o_ref[...] = jnp.dot(s, w_ref[...], preferred_element_type=F32) + b_ref[...]


def _mod(c, w_mod, b_mod):
    n, d = c.shape
    cols = w_mod.shape[1]
    tn = 2048
    return pl.pallas_call(
        _mod_kernel,
        grid=(cols // tn,),
        in_specs=[pl.BlockSpec((n, d), lambda j: (0, 0)),
                  pl.BlockSpec((d, tn), lambda j: (0, j)),
                  pl.BlockSpec((1, tn), lambda j: (0, j))],
        out_specs=pl.BlockSpec((n, tn), lambda j: (0, j)),
        out_shape=jax.ShapeDtypeStruct((n, cols), F32),
        compiler_params=_params(("arbitrary",)),
        name="mod",
    )(c, w_mod, b_mod.reshape(1, cols))


def _norm_rope(a, g, cos, sin, ones_bd, rot_bd):
    heads = a.shape[1] // HEAD_DIM
    ssq = jnp.dot((a * a).astype(BF16), ones_bd, preferred_element_type=F32)
    y = a * lax.rsqrt(ssq * (1.0 / HEAD_DIM) + EPS) * jnp.concatenate([g] * heads, axis=1)
    rot = jnp.dot(y.astype(BF16), rot_bd, preferred_element_type=F32)
    return y * jnp.concatenate([cos] * heads, axis=1) + rot * jnp.concatenate([sin] * heads, axis=1)


def _inproj_kernel(x_ref, sc_ref, sh_ref, g_ref, w_ref, wlow_ref, gq_ref, gk_ref, cos_ref, sin_ref, ones_ref, rot_ref,
                   att_ref, qk_ref, vr_ref, gate_ref, low_ref, h_scr):
    n = pl.program_id(2)

    @pl.when(n == 0)
    def _():
        x = x_ref[...]
        ms = jnp.mean(x * x, axis=-1, keepdims=True)
        h = x * lax.rsqrt(ms + EPS) * g_ref[...]
        hb = (h * (1.0 + sc_ref[...]) + sh_ref[...]).astype(BF16)
        h_scr[...] = hb
        low_ref[...] = jnp.dot(hb, wlow_ref[...], preferred_element_type=F32)

    def proj():
        return jnp.dot(h_scr[...], w_ref[...], preferred_element_type=F32)

    @pl.when(n < ATT_W // IN_TN)
    def _():
        att_ref[...] = _norm_rope(proj(), gq_ref[...], cos_ref[...], sin_ref[...], ones_ref[...],
                                  rot_ref[...]).astype(BF16)

    @pl.when(n == ATT_W // IN_TN)
    def _():
        acc = proj()
        att_ref[:, :ATT_KV_W] = _norm_rope(acc[:, :ATT_KV_W], gk_ref[...], cos_ref[...], sin_ref[...],
                                           ones_ref[:ATT_KV_W, :ATT_KV_W], rot_ref[:ATT_KV_W, :ATT_KV_W]).astype(BF16)
        att_ref[:, ATT_KV_W:] = acc[:, ATT_KV_W:].astype(BF16)

    @pl.when(n == ATT_TILES)
    def _():
        qk_ref[...] = proj() * (GLA_DK ** -0.5)

    @pl.when((n > ATT_TILES) & (n < QK_END))
    def _():
        qk_ref[...] = proj()

    @pl.when((n >= QK_END) & (n < VR_END))
    def _():
        vr_ref[...] = proj().astype(BF16)

    @pl.when(n >= VR_END)
    def _():
        gate_ref[...] = proj().astype(BF16)


def _inproj(x, mod6, g_mix, w_main, w_low, g_q, g_k, cos, sin, ones_bd, rot_bd):
    B, T, D = x.shape
    tm = min(TILES["inproj"], T)
    n_tiles = w_main.shape[1] // IN_TN
    gate_tiles = n_tiles - VR_END
    row = lambda b, i, n: (b, i, 0)
    vec = lambda b, i, n: (0, 0)
    tab = lambda b, i, n: (i, 0)
    return pl.pallas_call(
        _inproj_kernel,
        grid=(B, T // tm, n_tiles),
        in_specs=[
            pl.BlockSpec((None, tm, D), row),
            pl.BlockSpec((None, 1, D), lambda b, i, n: (b * 6 + 1, 0, 0)),
            pl.BlockSpec((None, 1, D), lambda b, i, n: (b * 6 + 0, 0, 0)),
            pl.BlockSpec((1, D), vec),
            pl.BlockSpec((D, IN_TN), lambda b, i, n: (0, n)),
            pl.BlockSpec((D, LANES), vec),
            pl.BlockSpec((1, HEAD_DIM), vec),
            pl.BlockSpec((1, HEAD_DIM), vec),
            pl.BlockSpec((tm, HEAD_DIM), tab),
            pl.BlockSpec((tm, HEAD_DIM), tab),
            pl.BlockSpec((IN_TN, IN_TN), vec),
            pl.BlockSpec((IN_TN, IN_TN), vec),
        ],
        out_specs=[
            pl.BlockSpec((None, tm, IN_TN), lambda b, i, n: (b, i, jnp.minimum(n, ATT_TILES - 1))),
            pl.BlockSpec((None, tm, IN_TN), lambda b, i, n: (b, i, jnp.clip(n - ATT_TILES, 0, QK_TILES - 1))),
            pl.BlockSpec((None, tm, IN_TN), lambda b, i, n: (b, i, jnp.clip(n - QK_END, 0, VR_TILES - 1))),
            pl.BlockSpec((None, tm, IN_TN), lambda b, i, n: (b, i, jnp.clip(n - VR_END, 0, gate_tiles - 1))),
            pl.BlockSpec((None, tm, LANES), row),
        ],
        out_shape=[
            jax.ShapeDtypeStruct((B, T, ATT_COLS), BF16),
            jax.ShapeDtypeStruct((B, T, QK_COLS), F32),
            jax.ShapeDtypeStruct((B, T, VR_COLS), BF16),
            jax.ShapeDtypeStruct((B, T, gate_tiles * IN_TN), BF16),
            jax.ShapeDtypeStruct((B, T, LANES), F32),
        ],
        scratch_shapes=[pltpu.VMEM((tm, D), BF16)],
        compiler_params=_params(("arbitrary", "arbitrary", "arbitrary")),
        name="inproj",
    )(x, mod6, mod6, g_mix, w_main, w_low, g_q, g_k, cos, sin, ones_bd, rot_bd)


def _attn_kernel(q_ref, k_ref, v_ref, o_ref, v_ext):
    @pl.when(pl.program_id(2) == 0)
    def _():
        v_ext[:, :HEAD_DIM] = v_ref[...]
        v_ext[:, HEAD_DIM:] = jnp.ones((v_ext.shape[0], HEAD_DIM), BF16)

    k = k_ref[...]
    v = v_ext[...]
    for g in range(ATT_GROUP):
        cs = slice(g * HEAD_DIM, (g + 1) * HEAD_DIM)
        for r0 in range(0, q_ref.shape[0], ATT_UNIT_ROWS):
            rows = slice(r0, r0 + ATT_UNIT_ROWS)
            s = lax.dot_general(q_ref[rows, cs], k, NT, preferred_element_type=F32)
            m = jnp.max(s, axis=-1, keepdims=True)
            p = jnp.exp2((s - m).astype(BF16))
            o = jnp.dot(p, v, preferred_element_type=F32)
            o_ref[rows, cs] = (o[:, :HEAD_DIM] / o[:, HEAD_DIM:]).astype(BF16)


def _attention(qkv):
    B, T, _ = qkv.shape
    tq = min(TILES["attn"], T)
    gw = ATT_GROUP * HEAD_DIM
    k0 = ATT_W // HEAD_DIM
    v0 = (ATT_W + ATT_KV_W) // HEAD_DIM
    return pl.pallas_call(
        _attn_kernel,
        grid=(B, ATT_KV_HEADS, T // tq),
        in_specs=[
            pl.BlockSpec((None, tq, gw), lambda b, kv, i: (b, i, kv)),
            pl.BlockSpec((None, T, HEAD_DIM), lambda b, kv, i: (b, 0, k0 + kv)),
            pl.BlockSpec((None, T, HEAD_DIM), lambda b, kv, i: (b, 0, v0 + kv)),
        ],
        out_specs=pl.BlockSpec((None, tq, gw), lambda b, kv, i: (b, i, kv)),
        out_shape=jax.ShapeDtypeStruct((B, T, ATT_W), BF16),
        scratch_shapes=[pltpu.VMEM((T, 2 * HEAD_DIM), BF16)],
        compiler_params=_params(("arbitrary", "arbitrary", "arbitrary")),
        name="attn",
    )(qkv, qkv, qkv)


def _log_sigmoid(z):
    return jnp.minimum(z, 0.0) - jnp.log(1.0 + jnp.exp(-jnp.abs(z)))


def _gla_scan(q_ref, k_ref, v_ref, low_ref, w_ref, b_ref, o_ref, s_scr, d, forward, sub):
    C = GLA_CHUNK
    R = sub * C
    t_i = lax.broadcasted_iota(jnp.int32, (R, R), 0)
    s_i = lax.broadcasted_iota(jnp.int32, (R, R), 1)
    same = (t_i // C) == (s_i // C)
    if forward:
        cum = same & (s_i <= t_i)
        keep = cum
    else:
        cum = same & (s_i >= t_i)
        keep = same & (s_i > t_i)
    cum = jnp.where(cum, 1.0, 0.0).astype(BF16)

    z = jnp.dot(low_ref[...].astype(BF16), w_ref[...], preferred_element_type=F32) + b_ref[...]
    la = _log_sigmoid(z) * (1.0 / GATE_NORM)
    hi = la.astype(BF16)
    r1 = la - hi.astype(F32)
    mid = r1.astype(BF16)
    lo = (r1 - mid.astype(F32)).astype(BF16)
    bc = (jnp.dot(cum, hi, preferred_element_type=F32) + jnp.dot(cum, mid, preferred_element_type=F32)
          + jnp.dot(cum, lo, preferred_element_type=F32))
    e_pos = jnp.exp(bc)
    e_neg = jnp.exp(-bc)
    order = range(sub) if forward else reversed(range(sub))
    chunks = []
    for c in order:
        bc_c = bc[c * C:(c + 1) * C, :]
        btot = bc_c[C - 1:C, :] if forward else bc_c[0:1, :]
        chunks.append((c, jnp.exp(btot - bc_c), jnp.exp(btot)))

    for h in range(GLA_HEADS):
        ks = slice(h * GLA_DK, (h + 1) * GLA_DK)
        vs = slice(h * GLA_DV, (h + 1) * GLA_DV)
        k = k_ref[:, ks]
        v = v_ref[:, vs]
        qe = (q_ref[:, ks] * e_pos[:, ks]).astype(BF16)
        ke = (k * e_neg[:, ks]).astype(BF16)
        a = lax.dot_general(qe, ke, NT, preferred_element_type=F32)
        a = jnp.where(keep, a, 0.0).astype(BF16)
        o_intra = jnp.dot(a, v, preferred_element_type=F32)
        st = s_scr[d, h]
        for c, e_dec, e_tot in chunks:
            rows = slice(c * C, (c + 1) * C)
            o_ref[rows, vs] = o_intra[rows] + lax.dot_general(qe[rows], st.astype(BF16), NT,
                                                              preferred_element_type=F32)
            kd = (k[rows] * e_dec[:, ks]).astype(BF16)
            st = st * e_tot[:, ks] + lax.dot_general(v[rows], kd, TN, preferred_element_type=F32)
        s_scr[d, h] = st


def _gla_kernel(qf, kf, vf, lf, qb, kb, vb, lb, wf, wb, bf_, bb_, of, ob, s_scr, *, sub):
    @pl.when(pl.program_id(1) == 0)
    def _():
        s_scr[...] = jnp.zeros_like(s_scr)

    _gla_scan(qf, kf, vf, lf, wf, bf_, of, s_scr, 0, True, sub)
    _gla_scan(qb, kb, vb, lb, wb, bb_, ob, s_scr, 1, False, sub)


def _gla(qk, vr, low, w_up_f, b_f, w_up_b, b_b):
    B, T, _ = qk.shape
    rb = min(TILES["gla_rows"], T)
    nb = T // rb
    fwd = lambda b, c: (b, c, 0)
    bwd = lambda b, c: (b, nb - 1 - c, 0)
    vec = lambda b, c: (0, 0)

    def streams(row_map):
        r = lambda col: (lambda b, c: (*row_map(b, c)[:2], col))
        return [pl.BlockSpec((None, rb, GLA_QK), r(0)),
                pl.BlockSpec((None, rb, GLA_QK), r(1)),
                pl.BlockSpec((None, rb, GLA_V), r(0)),
                pl.BlockSpec((None, rb, LANES), r(0))]

    return pl.pallas_call(
        functools.partial(_gla_kernel, sub=rb // GLA_CHUNK),
        grid=(B, nb),
        in_specs=streams(fwd) + streams(bwd) + [
            pl.BlockSpec((LANES, GLA_QK), vec), pl.BlockSpec((LANES, GLA_QK), vec),
            pl.BlockSpec((1, GLA_QK), vec), pl.BlockSpec((1, GLA_QK), vec)],
        out_specs=[pl.BlockSpec((None, rb, GLA_V), fwd), pl.BlockSpec((None, rb, GLA_V), bwd)],
        out_shape=[jax.ShapeDtypeStruct((B, T, GLA_V), F32)] * 2,
        scratch_shapes=[pltpu.VMEM((2, GLA_HEADS, GLA_DV, GLA_DK), F32)],
        compiler_params=_params(("arbitrary", "arbitrary")),
        name="gla",
    )(qk, qk, vr, low, qk, qk, vr, low, w_up_f, w_up_b, b_f, b_b)


def _merge_kernel(att_ref, of_ref, ob_ref, r_ref, ga_ref, gg_ref, x_ref, gt_ref, sc_ref, sh_ref, ggla_ref, gffn_ref,
                  wa_ref, wg_ref, wo_ref, x1_ref, h2_ref):
    parts = []
    for h in range(GLA_HEADS):
        vs = slice(h * GLA_DV, (h + 1) * GLA_DV)
        o = of_ref[:, vs] + ob_ref[:, vs]
        ms = jnp.mean(o * o, axis=-1, keepdims=True)
        y = o * lax.rsqrt(ms + EPS) * ggla_ref[:, vs]
        parts.append((y * _silu(r_ref[:, vs].astype(F32))).astype(BF16))
    gl = jnp.concatenate(parts, axis=-1)
    a = jnp.dot(att_ref[...], wa_ref[...], preferred_element_type=F32)
    g = jnp.dot(gl, wg_ref[...], preferred_element_type=F32)
    gate_a = jax.nn.sigmoid(ga_ref[...].astype(F32))
    gate_g = jax.nn.sigmoid(gg_ref[...].astype(F32))
    merged = (gate_a * a + gate_g * g).astype(BF16)
    out = jnp.dot(merged, wo_ref[...], preferred_element_type=F32)
    x1 = x_ref[...] + gt_ref[...] * out
    x1_ref[...] = x1
    ms = jnp.mean(x1 * x1, axis=-1, keepdims=True)
    h2 = x1 * lax.rsqrt(ms + EPS) * gffn_ref[...]
    h2_ref[...] = (h2 * (1.0 + sc_ref[...]) + sh_ref[...]).astype(BF16)


def _merge(att, o_f, o_b, vr, gates, x, mod6, g_gla, g_ffn, w_br_att, w_br_gla, w_out):
    B, T, D = x.shape
    tm = min(TILES["merge"], T)
    row = lambda b, i: (b, i, 0)
    vec = lambda b, i: (0, 0)
    resident = lambda shape: pl.BlockSpec(shape, vec, pipeline_mode=pl.Buffered(1))
    return pl.pallas_call(
        _merge_kernel,
        grid=(B, T // tm),
        in_specs=[
            pl.BlockSpec((None, tm, ATT_W), row),
            pl.BlockSpec((None, tm, GLA_V), row),
            pl.BlockSpec((None, tm, GLA_V), row),
            pl.BlockSpec((None, tm, GLA_V), lambda b, i: (b, i, 1)),
            pl.BlockSpec((None, tm, D), lambda b, i: (b, i, 0)),
            pl.BlockSpec((None, tm, D), lambda b, i: (b, i, 1)),
            pl.BlockSpec((None, tm, D), row),
            pl.BlockSpec((None, 1, D), lambda b, i: (b * 6 + 2, 0, 0)),
            pl.BlockSpec((None, 1, D), lambda b, i: (b * 6 + 4, 0, 0)),
            pl.BlockSpec((None, 1, D), lambda b, i: (b * 6 + 3, 0, 0)),
            pl.BlockSpec((1, GLA_V), vec),
            pl.BlockSpec((1, D), vec),
            resident((ATT_W, D)),
            resident((GLA_V, D)),
            resident((D, D)),
        ],
        out_specs=[pl.BlockSpec((None, tm, D), row), pl.BlockSpec((None, tm, D), row)],
        out_shape=[jax.ShapeDtypeStruct((B, T, D), F32), jax.ShapeDtypeStruct((B, T, D), BF16)],
        compiler_params=_params(("arbitrary", "arbitrary")),
        name="merge",
    )(att, o_f, o_b, vr, gates, gates, x, mod6, mod6, mod6, g_gla, g_ffn, w_br_att, w_br_gla, w_out)


def _ffn_kernel(h_ref, hp_ref, hn_ref, wv_ref, wg_ref, cv_ref, cg_ref, bv_ref, bg_ref, wd_ref, x1_ref, gt_ref,
                gf_ref, y_ref, hext, acc, *, tm):
    i = pl.program_id(1)
    j = pl.program_id(2)
    halo = BF16_ROWS

    @pl.when(j == 0)
    def _():
        has_prev = (i > 0).astype(F32)
        has_next = (i < pl.num_programs(1) - 1).astype(F32)
        hext[0:halo, :] = (hp_ref[...].astype(F32) * has_prev).astype(BF16)
        hext[halo:halo + tm, :] = h_ref[...]
        hext[halo + tm:, :] = (hn_ref[...].astype(F32) * has_next).astype(BF16)
        acc[...] = jnp.zeros_like(acc)

    he = hext[...]
    ext = tm + 2 * halo

    def conv(u, c_ref, b_ref):
        prev = pltpu.roll(u, 1, axis=0)[halo:halo + tm]
        nxt = pltpu.roll(u, ext - 1, axis=0)[halo:halo + tm]
        return c_ref[0:1, :] * prev + c_ref[1:2, :] * u[halo:halo + tm] + c_ref[2:3, :] * nxt + b_ref[...]

    val = conv(jnp.dot(he, wv_ref[...], preferred_element_type=F32), cv_ref, bv_ref)
    gate = conv(jnp.dot(he, wg_ref[...], preferred_element_type=F32), cg_ref, bg_ref)
    act = (_silu(gate) * val).astype(BF16)
    acc[...] += jnp.dot(act, wd_ref[...], preferred_element_type=F32)

    @pl.when(j == pl.num_programs(2) - 1)
    def _():
        x2 = x1_ref[...] + gt_ref[...] * acc[...]
        ms = jnp.mean(x2 * x2, axis=-1, keepdims=True)
        y_ref[...] = x2 * lax.rsqrt(ms + EPS) * gf_ref[...]


def _ffn(h2, x1, mod6, w_up, w_conv, b_conv, w_down, g_final):
    B, T, D = x1.shape
    d_ff = w_down.shape[0]
    tm = min(TILES["ffn"], T)
    tf = TILES["ffn_cols"]
    nj = d_ff // tf
    halo = BF16_ROWS
    row = lambda b, i, j: (b, i, 0)
    return pl.pallas_call(
        functools.partial(_ffn_kernel, tm=tm),
        grid=(B, T // tm, nj),
        in_specs=[
            pl.BlockSpec((None, tm, D), row),
            pl.BlockSpec((None, halo, D), lambda b, i, j: (b, jnp.maximum(i * (tm // halo) - 1, 0), 0)),
            pl.BlockSpec((None, halo, D), lambda b, i, j: (b, jnp.minimum((i + 1) * (tm // halo), T // halo - 1), 0)),
            pl.BlockSpec((D, tf), lambda b, i, j: (0, j)),
            pl.BlockSpec((D, tf), lambda b, i, j: (0, nj + j)),
            pl.BlockSpec((3, tf), lambda b, i, j: (0, j)),
            pl.BlockSpec((3, tf), lambda b, i, j: (0, nj + j)),
            pl.BlockSpec((1, tf), lambda b, i, j: (0, j)),
            pl.BlockSpec((1, tf), lambda b, i, j: (0, nj + j)),
            pl.BlockSpec((tf, D), lambda b, i, j: (j, 0)),
            pl.BlockSpec((None, tm, D), row),
            pl.BlockSpec((None, 1, D), lambda b, i, j: (b * 6 + 5, 0, 0)),
            pl.BlockSpec((1, D), lambda b, i, j: (0, 0)),
        ],
        out_specs=pl.BlockSpec((None, tm, D), row),
        out_shape=jax.ShapeDtypeStruct((B, T, D), F32),
        scratch_shapes=[pltpu.VMEM((tm + 2 * halo, D), BF16), pltpu.VMEM((tm, D), F32)],
        compiler_params=_params(("arbitrary", "arbitrary", "arbitrary")),
        name="ffn",
    )(h2, h2, h2, w_up, w_up, w_conv, w_conv, b_conv, b_conv, w_down, x1, mod6, g_final)


def _rope_tables(T):
    rows = T // GRID_W
    axis_dim = HEAD_DIM // 2
    row = jnp.repeat(jnp.arange(rows, dtype=F32), GRID_W)
    col = jnp.tile(jnp.arange(GRID_W, dtype=F32), rows)
    inv = ROPE_THETA ** (-jnp.arange(0, axis_dim, 2, dtype=F32) / axis_dim)
    ar = row[:, None] * inv
    ac = col[:, None] * inv
    ang = jnp.concatenate([ar, ar, ac, ac], axis=-1)
    return jnp.cos(ang), jnp.sin(ang)


def _rope_matrices():
    j = jnp.arange(IN_TN)
    src, dst = j[:, None], j[None, :]
    half = HEAD_DIM // 4
    lower = (dst % (2 * half)) < half
    rot = jnp.where(lower & (src == dst + half), -1.0, 0.0) + jnp.where(~lower & (src == dst - half), 1.0, 0.0)
    ones = jnp.where(src // HEAD_DIM == dst // HEAD_DIM, 1.0, 0.0)
    return ones.astype(BF16), rot.astype(BF16)


def _prep_weights(w_mod, b_mod, g_mix_norm, w_in, g_q, g_k, w_a_up_f, b_a_f, w_a_up_b, b_a_b, g_gla, w_br_att,
                  w_br_gla, w_out, g_ffn_norm, w_up, w_conv, b_conv, w_down, g_final):
    D = w_in.shape[1]
    low0 = ATT_COLS + QK_COLS + VR_COLS
    low1 = low0 + 2 * GLA_RANK
    wi = w_in[0]
    w_main = jnp.concatenate([wi[:, :low0], wi[:, low1:]], axis=1).astype(BF16)
    w_low = jnp.pad(wi[:, low0:low1], ((0, 0), (0, LANES - 2 * GLA_RANK))).astype(BF16)
    up_f = jnp.pad(w_a_up_f[0], ((0, LANES - GLA_RANK), (0, 0))).astype(BF16)
    up_b = jnp.pad(w_a_up_b[0], ((GLA_RANK, LANES - 2 * GLA_RANK), (0, 0))).astype(BF16)
    return dict(
        w_mod=w_mod[0].astype(BF16), b_mod=b_mod[0], g_mix=g_mix_norm[0].reshape(1, D),
        w_main=w_main, w_low=w_low, g_q=g_q[0].reshape(1, HEAD_DIM) * (HEAD_DIM ** -0.5 * LOG2_E),
        g_k=g_k[0].reshape(1, HEAD_DIM),
        up_f=up_f, up_b=up_b, b_f=b_a_f[0].reshape(1, GLA_QK), b_b=b_a_b[0].reshape(1, GLA_QK),
        g_gla=g_gla[0].reshape(1, GLA_V), w_br_att=w_br_att[0].astype(BF16), w_br_gla=w_br_gla[0].astype(BF16),
        w_out=w_out[0].astype(BF16), g_ffn=g_ffn_norm[0].reshape(1, D), w_up=w_up[0].astype(BF16),
        w_conv=w_conv[0], b_conv=b_conv[0].reshape(1, -1), w_down=w_down[0].astype(BF16),
        g_final=g_final.reshape(1, D))


def _trunk(x, mod, W):
    B, T, D = x.shape
    mod6 = mod.reshape(B * 6, 1, D)
    cos, sin = _rope_tables(T)
    ones_bd, rot_bd = _rope_matrices()
    qkv, qk, vr, gates, low = _inproj(x, mod6, W["g_mix"], W["w_main"], W["w_low"], W["g_q"], W["g_k"],
                                      cos, sin, ones_bd, rot_bd)
    att = _attention(qkv)
    o_f, o_b = _gla(qk, vr, low, W["up_f"], W["b_f"], W["up_b"], W["b_b"])
    x1, h2 = _merge(att, o_f, o_b, vr, gates, x, mod6, W["g_gla"], W["g_ffn"], W["w_br_att"], W["w_br_gla"],
                    W["w_out"])
    return _ffn(h2, x1, mod6, W["w_up"], W["w_conv"], W["b_conv"], W["w_down"], W["g_final"])


def kernel(x_prompt, x_sample, c_prompt, c_sample, w_mod, b_mod, g_mix_norm, w_in, g_q, g_k, w_a_up_f, b_a_f,
           w_a_up_b, b_a_b, g_gla, w_br_att, w_br_gla, w_out, g_ffn_norm, w_up, w_conv, b_conv, w_down, g_final):
    assert w_mod.shape[0] == 1, "single-layer trunk"
    W = _prep_weights(w_mod, b_mod, g_mix_norm, w_in, g_q, g_k, w_a_up_f, b_a_f, w_a_up_b, b_a_b, g_gla, w_br_att,
                      w_br_gla, w_out, g_ffn_norm, w_up, w_conv, b_conv, w_down, g_final)
    nb = x_prompt.shape[0]
    mod = _mod(jnp.concatenate([c_prompt, c_sample], axis=0), W["w_mod"], W["b_mod"])
    return _trunk(x_prompt, mod[:nb], W), _trunk(x_sample, mod[nb:], W)
```

```python
import functools

import jax
import jax.numpy as jnp
from jax import lax
from jax.experimental import pallas as pl
from jax.experimental.pallas import tpu as pltpu

F32 = jnp.float32
BF16 = jnp.bfloat16
EPS = 1e-6
LOG2_E = 1.4426950408889634

HEAD_DIM = 128
ATT_HEADS = 8
ATT_KV_HEADS = 2
ATT_GROUP = ATT_HEADS // ATT_KV_HEADS
ATT_W = ATT_HEADS * HEAD_DIM
ATT_KV_W = ATT_KV_HEADS * HEAD_DIM
GRID_W = 64
ROPE_THETA = 10000.0
GLA_HEADS = 4
GLA_DK = 128
GLA_DV = 256
GLA_QK = GLA_HEADS * GLA_DK
GLA_V = GLA_HEADS * GLA_DV
GLA_RANK = 16
GATE_NORM = 16.0
GLA_CHUNK = 64
LANES = 128
BF16_ROWS = 16
VMEM_LIMIT = 56 * 1024 * 1024

IN_TN = 512
ATT_COLS = ATT_W + 2 * ATT_KV_W
QK_COLS = 2 * GLA_QK
VR_COLS = 2 * GLA_V
ATT_TILES = ATT_COLS // IN_TN
QK_TILES = QK_COLS // IN_TN
VR_TILES = VR_COLS // IN_TN
QK_END = ATT_TILES + QK_TILES
VR_END = QK_END + VR_TILES

TILES = dict(inproj=1024, attn=512, gla_rows=256, merge=256, ffn=512, ffn_cols=512)
ATT_UNIT_ROWS = 128

NT = (((1,), (1,)), ((), ()))
TN = (((0,), (0,)), ((), ()))


def _params(sem):
    return pltpu.CompilerParams(dimension_semantics=sem, vmem_limit_bytes=VMEM_LIMIT)


def _silu(v):
    return v * jax.nn.sigmoid(v)


def _mod_kernel(c_ref, w_ref, b_ref, o_ref):
    s = _silu(c_ref[...]).astype(BF16)
    o_ref[...] = jnp.dot(s, w_ref[...], preferred_element_type=F32) + b_ref[...]


def _mod(c, w_mod, b_mod):
    n, d = c.shape
    cols = w_mod.shape[1]
    tn = 2048
    return pl.pallas_call(
        _mod_kernel,
        grid=(cols // tn,),
        in_specs=[pl.BlockSpec((n, d), lambda j: (0, 0)),
                  pl.BlockSpec((d, tn), lambda j: (0, j)),
                  pl.BlockSpec((1, tn), lambda j: (0, j))],
        out_specs=pl.BlockSpec((n, tn), lambda j: (0, j)),
        out_shape=jax.ShapeDtypeStruct((n, cols), F32),
        compiler_params=_params(("arbitrary",)),
        name="mod",
    )(c, w_mod, b_mod.reshape(1, cols))


def _norm_rope(a, g, cos, sin, ones_bd, rot_bd):
    heads = a.shape[1] // HEAD_DIM
    ssq = jnp.dot((a * a).astype(BF16), ones_bd, preferred_element_type=F32)
    y = a * lax.rsqrt(ssq * (1.0 / HEAD_DIM) + EPS) * jnp.concatenate([g] * heads, axis=1)
    rot = jnp.dot(y.astype(BF16), rot_bd, preferred_element_type=F32)
    return y * jnp.concatenate([cos] * heads, axis=1) + rot * jnp.concatenate([sin] * heads, axis=1)


def _inproj_kernel(x_ref, sc_ref, sh_ref, g_ref, w_ref, wlow_ref, gq_ref, gk_ref, cos_ref, sin_ref, ones_ref, rot_ref,
                   att_ref, qk_ref, vr_ref, gate_ref, low_ref, h_scr):
    n = pl.program_id(2)

    @pl.when(n == 0)
    def _():
        x = x_ref[...]
        ms = jnp.mean(x * x, axis=-1, keepdims=True)
        h = x * lax.rsqrt(ms + EPS) * g_ref[...]
        hb = (h * (1.0 + sc_ref[...]) + sh_ref[...]).astype(BF16)
        h_scr[...] = hb
        low_ref[...] = jnp.dot(hb, wlow_ref[...], preferred_element_type=F32)

    def proj():
        return jnp.dot(h_scr[...], w_ref[...], preferred_element_type=F32)

    @pl.when(n < ATT_W // IN_TN)
    def _():
        att_ref[...] = _norm_rope(proj(), gq_ref[...], cos_ref[...], sin_ref[...], ones_ref[...],
                                  rot_ref[...]).astype(BF16)

    @pl.when(n == ATT_W // IN_TN)
    def _():
        acc = proj()
        att_ref[:, :ATT_KV_W] = _norm_rope(acc[:, :ATT_KV_W], gk_ref[...], cos_ref[...], sin_ref[...],
                                           ones_ref[:ATT_KV_W, :ATT_KV_W], rot_ref[:ATT_KV_W, :ATT_KV_W]).astype(BF16)
        att_ref[:, ATT_KV_W:] = acc[:, ATT_KV_W:].astype(BF16)

    @pl.when(n == ATT_TILES)
    def _():
        qk_ref[...] = proj() * (GLA_DK ** -0.5)

    @pl.when((n > ATT_TILES) & (n < QK_END))
    def _():
        qk_ref[...] = proj()

    @pl.when((n >= QK_END) & (n < VR_END))
    def _():
        vr_ref[...] = proj().astype(BF16)

    @pl.when(n >= VR_END)
    def _():
        gate_ref[...] = proj().astype(BF16)


def _inproj(x, mod6, g_mix, w_main, w_low, g_q, g_k, cos, sin, ones_bd, rot_bd):
    B, T, D = x.shape
    tm = min(TILES["inproj"], T)
    n_tiles = w_main.shape[1] // IN_TN
    gate_tiles = n_tiles - VR_END
    row = lambda b, i, n: (b, i, 0)
    vec = lambda b, i, n: (0, 0)
    tab = lambda b, i, n: (i, 0)
    return pl.pallas_call(
        _inproj_kernel,
        grid=(B, T // tm, n_tiles),
        in_specs=[
            pl.BlockSpec((None, tm, D), row),
            pl.BlockSpec((None, 1, D), lambda b, i, n: (b * 6 + 1, 0, 0)),
            pl.BlockSpec((None, 1, D), lambda b, i, n: (b * 6 + 0, 0, 0)),
            pl.BlockSpec((1, D), vec),
            pl.BlockSpec((D, IN_TN), lambda b, i, n: (0, n)),
            pl.BlockSpec((D, LANES), vec),
            pl.BlockSpec((1, HEAD_DIM), vec),
            pl.BlockSpec((1, HEAD_DIM), vec),
            pl.BlockSpec((tm, HEAD_DIM), tab),
            pl.BlockSpec((tm, HEAD_DIM), tab),
            pl.BlockSpec((IN_TN, IN_TN), vec),
            pl.BlockSpec((IN_TN, IN_TN), vec),
        ],
        out_specs=[
            pl.BlockSpec((None, tm, IN_TN), lambda b, i, n: (b, i, jnp.minimum(n, ATT_TILES - 1))),
            pl.BlockSpec((None, tm, IN_TN), lambda b, i, n: (b, i, jnp.clip(n - ATT_TILES, 0, QK_TILES - 1))),
            pl.BlockSpec((None, tm, IN_TN), lambda b, i, n: (b, i, jnp.clip(n - QK_END, 0, VR_TILES - 1))),
            pl.BlockSpec((None, tm, IN_TN), lambda b, i, n: (b, i, jnp.clip(n - VR_END, 0, gate_tiles - 1))),
            pl.BlockSpec((None, tm, LANES), row),
        ],
        out_shape=[
            jax.ShapeDtypeStruct((B, T, ATT_COLS), BF16),
            jax.ShapeDtypeStruct((B, T, QK_COLS), F32),
            jax.ShapeDtypeStruct((B, T, VR_COLS), BF16),
            jax.ShapeDtypeStruct((B, T, gate_tiles * IN_TN), BF16),
            jax.ShapeDtypeStruct((B, T, LANES), F32),
        ],
        scratch_shapes=[pltpu.VMEM((tm, D), BF16)],
        compiler_params=_params(("arbitrary", "arbitrary", "arbitrary")),
        name="inproj",
    )(x, mod6, mod6, g_mix, w_main, w_low, g_q, g_k, cos, sin, ones_bd, rot_bd)


def _attn_kernel(q_ref, k_ref, v_ref, o_ref, v_ext):
    @pl.when(pl.program_id(2) == 0)
    def _():
        v_ext[:, :HEAD_DIM] = v_ref[...]
        v_ext[:, HEAD_DIM:] = jnp.ones((v_ext.shape[0], HEAD_DIM), BF16)

    k = k_ref[...]
    v = v_ext[...]
    for g in range(ATT_GROUP):
        cs = slice(g * HEAD_DIM, (g + 1) * HEAD_DIM)
        for r0 in range(0, q_ref.shape[0], ATT_UNIT_ROWS):
            rows = slice(r0, r0 + ATT_UNIT_ROWS)
            s = lax.dot_general(q_ref[rows, cs], k, NT, preferred_element_type=F32)
            m = jnp.max(s, axis=-1, keepdims=True)
            p = jnp.exp2((s - m).astype(BF16))
            o = jnp.dot(p, v, preferred_element_type=F32)
            o_ref[rows, cs] = (o[:, :HEAD_DIM] / o[:, HEAD_DIM:]).astype(BF16)


def _attention(qkv):
    B, T, _ = qkv.shape
    tq = min(TILES["attn"], T)
    gw = ATT_GROUP * HEAD_DIM
    k0 = ATT_W // HEAD_DIM
    v0 = (ATT_W + ATT_KV_W) // HEAD_DIM
    return pl.pallas_call(
        _attn_kernel,
        grid=(B, ATT_KV_HEADS, T // tq),
        in_specs=[
            pl.BlockSpec((None, tq, gw), lambda b, kv, i: (b, i, kv)),
            pl.BlockSpec((None, T, HEAD_DIM), lambda b, kv, i: (b, 0, k0 + kv)),
            pl.BlockSpec((None, T, HEAD_DIM), lambda b, kv, i: (b, 0, v0 + kv)),
        ],
        out_specs=pl.BlockSpec((None, tq, gw), lambda b, kv, i: (b, i, kv)),
        out_shape=jax.ShapeDtypeStruct((B, T, ATT_W), BF16),
        scratch_shapes=[pltpu.VMEM((T, 2 * HEAD_DIM), BF16)],
        compiler_params=_params(("arbitrary", "arbitrary", "arbitrary")),
        name="attn",
    )(qkv, qkv, qkv)


def _log_sigmoid(z):
    return jnp.minimum(z, 0.0) - jnp.log(1.0 + jnp.exp(-jnp.abs(z)))


def _gla_scan(q_ref, k_ref, v_ref, low_ref, w_ref, b_ref, o_ref, s_scr, d, forward, sub):
    C = GLA_CHUNK
    R = sub * C
    t_i = lax.broadcasted_iota(jnp.int32, (R, R), 0)
    s_i = lax.broadcasted_iota(jnp.int32, (R, R), 1)
    same = (t_i // C) == (s_i // C)
    if forward:
        cum = same & (s_i <= t_i)
        keep = cum
    else:
        cum = same & (s_i >= t_i)
        keep = same & (s_i > t_i)
    cum = jnp.where(cum, 1.0, 0.0).astype(BF16)

    z = jnp.dot(low_ref[...].astype(BF16), w_ref[...], preferred_element_type=F32) + b_ref[...]
    la = _log_sigmoid(z) * (LOG2_E / GATE_NORM)
    hi = la.astype(BF16)
    r1 = la - hi.astype(F32)
    mid = r1.astype(BF16)
    lo = (r1 - mid.astype(F32)).astype(BF16)
    bc = (jnp.dot(cum, hi, preferred_element_type=F32) + jnp.dot(cum, mid, preferred_element_type=F32)
          + jnp.dot(cum, lo, preferred_element_type=F32))
    e_pos = jnp.exp2(bc)
    e_neg = jnp.exp2(-bc)
    order = range(sub) if forward else reversed(range(sub))
    chunks = []
    for c in order:
        bc_c = bc[c * C:(c + 1) * C, :]
        btot = bc_c[C - 1:C, :] if forward else bc_c[0:1, :]
        chunks.append((c, jnp.exp2(btot - bc_c), jnp.exp2(btot)))

    for h in range(GLA_HEADS):
        ks = slice(h * GLA_DK, (h + 1) * GLA_DK)
        vs = slice(h * GLA_DV, (h + 1) * GLA_DV)
        k = k_ref[:, ks]
        v = v_ref[:, vs]
        qe = (q_ref[:, ks] * e_pos[:, ks]).astype(BF16)
        ke = (k * e_neg[:, ks]).astype(BF16)
        a = lax.dot_general(qe, ke, NT, preferred_element_type=F32)
        a = jnp.where(keep, a, 0.0).astype(BF16)
        o_intra = jnp.dot(a, v, preferred_element_type=F32)
        st = s_scr[d, h]
        for c, e_dec, e_tot in chunks:
            rows = slice(c * C, (c + 1) * C)
            o_ref[rows, vs] = o_intra[rows] + lax.dot_general(qe[rows], st.astype(BF16), NT,
                                                              preferred_element_type=F32)
            kd = (k[rows] * e_dec[:, ks]).astype(BF16)
            st = st * e_tot[:, ks] + lax.dot_general(v[rows], kd, TN, preferred_element_type=F32)
        s_scr[d, h] = st


def _gla_kernel(qf, kf, vf, lf, qb, kb, vb, lb, wf, wb, bf_, bb_, of, ob, s_scr, *, sub):
    @pl.when(pl.program_id(1) == 0)
    def _():
        s_scr[...] = jnp.zeros_like(s_scr)

    _gla_scan(qf, kf, vf, lf, wf, bf_, of, s_scr, 0, True, sub)
    _gla_scan(qb, kb, vb, lb, wb, bb_, ob, s_scr, 1, False, sub)


def _gla(qk, vr, low, w_up_f, b_f, w_up_b, b_b):
    B, T, _ = qk.shape
    rb = min(TILES["gla_rows"], T)
    nb = T // rb
    fwd = lambda b, c: (b, c, 0)
    bwd = lambda b, c: (b, nb - 1 - c, 0)
    vec = lambda b, c: (0, 0)

    def streams(row_map):
        r = lambda col: (lambda b, c: (*row_map(b, c)[:2], col))
        return [pl.BlockSpec((None, rb, GLA_QK), r(0)),
                pl.BlockSpec((None, rb, GLA_QK), r(1)),
                pl.BlockSpec((None, rb, GLA_V), r(0)),
                pl.BlockSpec((None, rb, LANES), r(0))]

    return pl.pallas_call(
        functools.partial(_gla_kernel, sub=rb // GLA_CHUNK),
        grid=(B, nb),
        in_specs=streams(fwd) + streams(bwd) + [
            pl.BlockSpec((LANES, GLA_QK), vec), pl.BlockSpec((LANES, GLA_QK), vec),
            pl.BlockSpec((1, GLA_QK), vec), pl.BlockSpec((1, GLA_QK), vec)],
        out_specs=[pl.BlockSpec((None, rb, GLA_V), fwd), pl.BlockSpec((None, rb, GLA_V), bwd)],
        out_shape=[jax.ShapeDtypeStruct((B, T, GLA_V), F32)] * 2,
        scratch_shapes=[pltpu.VMEM((2, GLA_HEADS, GLA_DV, GLA_DK), F32)],
        compiler_params=_params(("arbitrary", "arbitrary")),
        name="gla",
    )(qk, qk, vr, low, qk, qk, vr, low, w_up_f, w_up_b, b_f, b_b)


def _merge_kernel(att_ref, of_ref, ob_ref, r_ref, ga_ref, gg_ref, x_ref, gt_ref, sc_ref, sh_ref, ggla_ref, gffn_ref,
                  wa_ref, wg_ref, wo_ref, x1_ref, h2_ref):
    parts = []
    for h in range(GLA_HEADS):
        vs = slice(h * GLA_DV, (h + 1) * GLA_DV)
        o = of_ref[:, vs] + ob_ref[:, vs]
        ms = jnp.mean(o * o, axis=-1, keepdims=True)
        y = o * lax.rsqrt(ms + EPS) * ggla_ref[:, vs]
        parts.append((y * _silu(r_ref[:, vs].astype(F32))).astype(BF16))
    gl = jnp.concatenate(parts, axis=-1)
    a = jnp.dot(att_ref[...], wa_ref[...], preferred_element_type=F32)
    g = jnp.dot(gl, wg_ref[...], preferred_element_type=F32)
    gate_a = jax.nn.sigmoid(ga_ref[...].astype(F32))
    gate_g = jax.nn.sigmoid(gg_ref[...].astype(F32))
    merged = (gate_a * a + gate_g * g).astype(BF16)
    out = jnp.dot(merged, wo_ref[...], preferred_element_type=F32)
    x1 = x_ref[...] + gt_ref[...] * out
    x1_ref[...] = x1
    ms = jnp.mean(x1 * x1, axis=-1, keepdims=True)
    h2 = x1 * lax.rsqrt(ms + EPS) * gffn_ref[...]
    h2_ref[...] = (h2 * (1.0 + sc_ref[...]) + sh_ref[...]).astype(BF16)


def _merge(att, o_f, o_b, vr, gates, x, mod6, g_gla, g_ffn, w_br_att, w_br_gla, w_out):
    B, T, D = x.shape
    tm = min(TILES["merge"], T)
    row = lambda b, i: (b, i, 0)
    vec = lambda b, i: (0, 0)
    resident = lambda shape: pl.BlockSpec(shape, vec, pipeline_mode=pl.Buffered(1))
    return pl.pallas_call(
        _merge_kernel,
        grid=(B, T // tm),
        in_specs=[
            pl.BlockSpec((None, tm, ATT_W), row),
            pl.BlockSpec((None, tm, GLA_V), row),
            pl.BlockSpec((None, tm, GLA_V), row),
            pl.BlockSpec((None, tm, GLA_V), lambda b, i: (b, i, 1)),
            pl.BlockSpec((None, tm, D), lambda b, i: (b, i, 0)),
            pl.BlockSpec((None, tm, D), lambda b, i: (b, i, 1)),
            pl.BlockSpec((None, tm, D), row),
            pl.BlockSpec((None, 1, D), lambda b, i: (b * 6 + 2, 0, 0)),
            pl.BlockSpec((None, 1, D), lambda b, i: (b * 6 + 4, 0, 0)),
            pl.BlockSpec((None, 1, D), lambda b, i: (b * 6 + 3, 0, 0)),
            pl.BlockSpec((1, GLA_V), vec),
            pl.BlockSpec((1, D), vec),
            resident((ATT_W, D)),
            resident((GLA_V, D)),
            resident((D, D)),
        ],
        out_specs=[pl.BlockSpec((None, tm, D), row), pl.BlockSpec((None, tm, D), row)],
        out_shape=[jax.ShapeDtypeStruct((B, T, D), F32), jax.ShapeDtypeStruct((B, T, D), BF16)],
        compiler_params=_params(("arbitrary", "arbitrary")),
        name="merge",
    )(att, o_f, o_b, vr, gates, gates, x, mod6, mod6, mod6, g_gla, g_ffn, w_br_att, w_br_gla, w_out)


def _ffn_kernel(h_ref, hp_ref, hn_ref, wv_ref, wg_ref, cv_ref, cg_ref, bv_ref, bg_ref, wd_ref, x1_ref, gt_ref,
                gf_ref, perm_ref, y_ref, hext, acc, *, tm):
    i = pl.program_id(1)
    j = pl.program_id(2)
    halo = BF16_ROWS
    sub = 8
    G = tm // sub
    lane_tiles = acc.shape[0]

    def lanes(c):
        return slice(c * LANES, (c + 1) * LANES)

    def gather_rows(ref, start, stride):
        return jnp.concatenate([ref[c, pl.ds(start, sub, stride=stride), :] for c in range(lane_tiles)], axis=1)

    @pl.when(j == 0)
    def _():
        has_prev = (i > 0).astype(F32)
        has_next = (i < pl.num_programs(1) - 1).astype(F32)
        hext[0:halo, :] = (hp_ref[...].astype(F32) * has_prev).astype(BF16)
        hext[halo + tm:, :] = (hn_ref[...].astype(F32) * has_next).astype(BF16)
        hext[halo:halo + tm, :] = jnp.dot(perm_ref[...], h_ref[...], preferred_element_type=F32).astype(BF16)
        acc[...] = jnp.zeros_like(acc)

    he = hext[...]
    row_id = lax.broadcasted_iota(jnp.int32, (sub, wv_ref.shape[1]), 0)

    def conv(u, c_ref, b_ref):
        main = u[halo:halo + tm]
        before = u[halo - 1:halo]
        after = u[halo + tm:halo + tm + 1]
        first = jnp.where(row_id == 0, before, pltpu.roll(main[tm - sub:], 1, axis=0))
        last = jnp.where(row_id == sub - 1, after, pltpu.roll(main[:sub], sub - 1, axis=0))
        prev = jnp.concatenate([first, main[:tm - sub]], axis=0)
        nxt = jnp.concatenate([main[sub:], last], axis=0)
        return c_ref[0:1, :] * prev + c_ref[1:2, :] * main + c_ref[2:3, :] * nxt + b_ref[...]

    val = conv(jnp.dot(he, wv_ref[...], preferred_element_type=F32), cv_ref, bv_ref)
    gate = conv(jnp.dot(he, wg_ref[...], preferred_element_type=F32), cg_ref, bg_ref)
    act = (_silu(gate) * val).astype(BF16)
    down = jnp.dot(act, wd_ref[...], preferred_element_type=F32)
    for c in range(lane_tiles):
        acc[c] += down[:, lanes(c)]

    @pl.when(j == pl.num_programs(2) - 1)
    def _():
        for g in range(G):
            rows = slice(g * sub, (g + 1) * sub)
            ffn = gather_rows(acc, (g % (G // sub)) * sub * sub + g // (G // sub), sub)
            x2 = x1_ref[rows, :] + gt_ref[...] * ffn
            ms = jnp.mean(x2 * x2, axis=-1, keepdims=True)
            y_ref[rows, :] = x2 * lax.rsqrt(ms + EPS) * gf_ref[...]


def _row_permutation(tm):
    pos = jnp.arange(tm)
    token = (pos % 8) * (tm // 8) + pos // 8
    return (token[:, None] == jnp.arange(tm)[None, :]).astype(BF16)


def _ffn(h2, x1, mod6, w_up, w_conv, b_conv, w_down, g_final):
    B, T, D = x1.shape
    d_ff = w_down.shape[0]
    tm = min(TILES["ffn"], T)
    tf = TILES["ffn_cols"]
    nj = d_ff // tf
    halo = BF16_ROWS
    row = lambda b, i, j: (b, i, 0)
    return pl.pallas_call(
        functools.partial(_ffn_kernel, tm=tm),
        grid=(B, T // tm, nj),
        in_specs=[
            pl.BlockSpec((None, tm, D), row),
            pl.BlockSpec((None, halo, D), lambda b, i, j: (b, jnp.maximum(i * (tm // halo) - 1, 0), 0)),
            pl.BlockSpec((None, halo, D), lambda b, i, j: (b, jnp.minimum((i + 1) * (tm // halo), T // halo - 1), 0)),
            pl.BlockSpec((D, tf), lambda b, i, j: (0, j)),
            pl.BlockSpec((D, tf), lambda b, i, j: (0, nj + j)),
            pl.BlockSpec((3, tf), lambda b, i, j: (0, j)),
            pl.BlockSpec((3, tf), lambda b, i, j: (0, nj + j)),
            pl.BlockSpec((1, tf), lambda b, i, j: (0, j)),
            pl.BlockSpec((1, tf), lambda b, i, j: (0, nj + j)),
            pl.BlockSpec((tf, D), lambda b, i, j: (j, 0)),
            pl.BlockSpec((None, tm, D), row),
            pl.BlockSpec((None, 1, D), lambda b, i, j: (b * 6 + 5, 0, 0)),
            pl.BlockSpec((1, D), lambda b, i, j: (0, 0)),
            pl.BlockSpec((tm, tm), lambda b, i, j: (0, 0)),
        ],
        out_specs=pl.BlockSpec((None, tm, D), row),
        out_shape=jax.ShapeDtypeStruct((B, T, D), F32),
        scratch_shapes=[pltpu.VMEM((tm + 2 * halo, D), BF16),
                        pltpu.VMEM((D // LANES, tm, LANES), F32)],
        compiler_params=_params(("arbitrary", "arbitrary", "arbitrary")),
        name="ffn",
    )(h2, h2, h2, w_up, w_up, w_conv, w_conv, b_conv, b_conv, w_down, x1, mod6, g_final, _row_permutation(tm))


def _rope_tables(T):
    rows = T // GRID_W
    axis_dim = HEAD_DIM // 2
    row = jnp.repeat(jnp.arange(rows, dtype=F32), GRID_W)
    col = jnp.tile(jnp.arange(GRID_W, dtype=F32), rows)
    inv = ROPE_THETA ** (-jnp.arange(0, axis_dim, 2, dtype=F32) / axis_dim)
    ar = row[:, None] * inv
    ac = col[:, None] * inv
    ang = jnp.concatenate([ar, ar, ac, ac], axis=-1)
    return jnp.cos(ang), jnp.sin(ang)


def _rope_matrices():
    j = jnp.arange(IN_TN)
    src, dst = j[:, None], j[None, :]
    half = HEAD_DIM // 4
    lower = (dst % (2 * half)) < half
    rot = jnp.where(lower & (src == dst + half), -1.0, 0.0) + jnp.where(~lower & (src == dst - half), 1.0, 0.0)
    ones = jnp.where(src // HEAD_DIM == dst // HEAD_DIM, 1.0, 0.0)
    return ones.astype(BF16), rot.astype(BF16)


def _prep_weights(w_mod, b_mod, g_mix_norm, w_in, g_q, g_k, w_a_up_f, b_a_f, w_a_up_b, b_a_b, g_gla, w_br_att,
                  w_br_gla, w_out, g_ffn_norm, w_up, w_conv, b_conv, w_down, g_final):
    D = w_in.shape[1]
    low0 = ATT_COLS + QK_COLS + VR_COLS
    low1 = low0 + 2 * GLA_RANK
    wi = w_in[0]
    w_main = jnp.concatenate([wi[:, :low0], wi[:, low1:]], axis=1).astype(BF16)
    w_low = jnp.pad(wi[:, low0:low1], ((0, 0), (0, LANES - 2 * GLA_RANK))).astype(BF16)
    up_f = jnp.pad(w_a_up_f[0], ((0, LANES - GLA_RANK), (0, 0))).astype(BF16)
    up_b = jnp.pad(w_a_up_b[0], ((GLA_RANK, LANES - 2 * GLA_RANK), (0, 0))).astype(BF16)
    return dict(
        w_mod=w_mod[0].astype(BF16), b_mod=b_mod[0], g_mix=g_mix_norm[0].reshape(1, D),
        w_main=w_main, w_low=w_low, g_q=g_q[0].reshape(1, HEAD_DIM) * (HEAD_DIM ** -0.5 * LOG2_E),
        g_k=g_k[0].reshape(1, HEAD_DIM),
        up_f=up_f, up_b=up_b, b_f=b_a_f[0].reshape(1, GLA_QK), b_b=b_a_b[0].reshape(1, GLA_QK),
        g_gla=g_gla[0].reshape(1, GLA_V), w_br_att=w_br_att[0].astype(BF16), w_br_gla=w_br_gla[0].astype(BF16),
        w_out=w_out[0].astype(BF16), g_ffn=g_ffn_norm[0].reshape(1, D), w_up=w_up[0].astype(BF16),
        w_conv=w_conv[0], b_conv=b_conv[0].reshape(1, -1), w_down=w_down[0].astype(BF16),
        g_final=g_final.reshape(1, D))


def _trunk(x, mod, W):
    B, T, D = x.shape
    mod6 = mod.reshape(B * 6, 1, D)
    cos, sin = _rope_tables(T)
    ones_bd, rot_bd = _rope_matrices()
    qkv, qk, vr, gates, low = _inproj(x, mod6, W["g_mix"], W["w_main"], W["w_low"], W["g_q"], W["g_k"],
                                      cos, sin, ones_bd, rot_bd)
    att = _attention(qkv)
    o_f, o_b = _gla(qk, vr, low, W["up_f"], W["b_f"], W["up_b"], W["b_b"])
    x1, h2 = _merge(att, o_f, o_b, vr, gates, x, mod6, W["g_gla"], W["g_ffn"], W["w_br_att"], W["w_br_gla"],
                    W["w_out"])
    return _ffn(h2, x1, mod6, W["w_up"], W["w_conv"], W["b_conv"], W["w_down"], W["g_final"])


def kernel(x_prompt, x_sample, c_prompt, c_sample, w_mod, b_mod, g_mix_norm, w_in, g_q, g_k, w_a_up_f, b_a_f,
           w_a_up_b, b_a_b, g_gla, w_br_att, w_br_gla, w_out, g_ffn_norm, w_up, w_conv, b_conv, w_down, g_final):
    assert w_mod.shape[0] == 1, "single-layer trunk"
    W = _prep_weights(w_mod, b_mod, g_mix_norm, w_in, g_q, g_k, w_a_up_f, b_a_f, w_a_up_b, b_a_b, g_gla, w_br_att,
                      w_br_gla, w_out, g_ffn_norm, w_up, w_conv, b_conv, w_down, g_final)
    nb = x_prompt.shape[0]
    mod = _mod(jnp.concatenate([c_prompt, c_sample], axis=0), W["w_mod"], W["b_mod"])
    return _trunk(x_prompt, mod[:nb], W), _trunk(x_sample, mod[nb:], W)
```

```python
import functools

import jax
import jax.numpy as jnp
from jax import lax
from jax.experimental import pallas as pl
from jax.experimental.pallas import tpu as pltpu

F32 = jnp.float32
BF16 = jnp.bfloat16
EPS = 1e-6
LOG2_E = 1.4426950408889634

HEAD_DIM = 128
ATT_HEADS = 8
ATT_KV_HEADS = 2
ATT_GROUP = ATT_HEADS // ATT_KV_HEADS
ATT_W = ATT_HEADS * HEAD_DIM
ATT_KV_W = ATT_KV_HEADS * HEAD_DIM
GRID_W = 64
ROPE_THETA = 10000.0
GLA_HEADS = 4
GLA_DK = 128
GLA_DV = 256
GLA_QK = GLA_HEADS * GLA_DK
GLA_V = GLA_HEADS * GLA_DV
GLA_RANK = 16
GATE_NORM = 16.0
GLA_CHUNK = 64
LANES = 128
BF16_ROWS = 16
VMEM_LIMIT = 56 * 1024 * 1024

IN_TN = 512
ATT_COLS = ATT_W + 2 * ATT_KV_W
QK_COLS = 2 * GLA_QK
VR_COLS = 2 * GLA_V
ATT_TILES = ATT_COLS // IN_TN
QK_TILES = QK_COLS // IN_TN
VR_TILES = VR_COLS // IN_TN
QK_END = ATT_TILES + QK_TILES
VR_END = QK_END + VR_TILES

TILES = dict(inproj=1024, attn=512, gla_rows=256, merge=256, ffn=512, ffn_cols=512)
ATT_UNIT_ROWS = 128

NT = (((1,), (1,)), ((), ()))
TN = (((0,), (0,)), ((), ()))


def _params(sem):
    return pltpu.CompilerParams(dimension_semantics=sem, vmem_limit_bytes=VMEM_LIMIT)


def _silu(v):
    return v * jax.nn.sigmoid(v)


def _tile_ahead(b, i, ahead, n_batch, n_tiles):
    lin = jnp.clip(b * n_tiles + i + ahead, 0, n_batch * n_tiles - 1)
    return lin // n_tiles, lin % n_tiles


def _mod_kernel(c_ref, w_ref, b_ref, o_ref):
    s = _silu(c_ref[...]).astype(BF16)
    o_ref[...] = jnp.dot(s, w_ref[...], preferred_element_type=F32) + b_ref[...]


def _mod(c, w_mod, b_mod):
    n, d = c.shape
    cols = w_mod.shape[1]
    tn = 2048
    return pl.pallas_call(
        _mod_kernel,
        grid=(cols // tn,),
        in_specs=[pl.BlockSpec((n, d), lambda j: (0, 0)),
                  pl.BlockSpec((d, tn), lambda j: (0, j)),
                  pl.BlockSpec((1, tn), lambda j: (0, j))],
        out_specs=pl.BlockSpec((n, tn), lambda j: (0, j)),
        out_shape=jax.ShapeDtypeStruct((n, cols), F32),
        compiler_params=_params(("arbitrary",)),
        name="mod",
    )(c, w_mod, b_mod.reshape(1, cols))


def _norm_rope(a, g, cos, sin, ones_bd, rot_bd):
    heads = a.shape[1] // HEAD_DIM
    ssq = jnp.dot((a * a).astype(BF16), ones_bd, preferred_element_type=F32)
    y = a * lax.rsqrt(ssq * (1.0 / HEAD_DIM) + EPS) * jnp.concatenate([g] * heads, axis=1)
    rot = jnp.dot(y.astype(BF16), rot_bd, preferred_element_type=F32)
    return y * jnp.concatenate([cos] * heads, axis=1) + rot * jnp.concatenate([sin] * heads, axis=1)


def _inproj_kernel(x_ref, sc_ref, sh_ref, g_ref, w_ref, wlow_ref, gq_ref, gk_ref, cos_ref, sin_ref, ones_ref, rot_ref,
                   att_ref, qk_ref, vr_ref, gate_ref, low_ref, h_scr):
    n = pl.program_id(2)

    @pl.when(n == 0)
    def _():
        x = x_ref[...]
        ms = jnp.mean(x * x, axis=-1, keepdims=True)
        h = x * lax.rsqrt(ms + EPS) * g_ref[...]
        hb = (h * (1.0 + sc_ref[...]) + sh_ref[...]).astype(BF16)
        h_scr[...] = hb
        low_ref[...] = jnp.dot(hb, wlow_ref[...], preferred_element_type=F32)

    def proj():
        return jnp.dot(h_scr[...], w_ref[...], preferred_element_type=F32)

    @pl.when(n < ATT_W // IN_TN)
    def _():
        att_ref[...] = _norm_rope(proj(), gq_ref[...], cos_ref[...], sin_ref[...], ones_ref[...],
                                  rot_ref[...]).astype(BF16)

    @pl.when(n == ATT_W // IN_TN)
    def _():
        acc = proj()
        att_ref[:, :ATT_KV_W] = _norm_rope(acc[:, :ATT_KV_W], gk_ref[...], cos_ref[...], sin_ref[...],
                                           ones_ref[:ATT_KV_W, :ATT_KV_W], rot_ref[:ATT_KV_W, :ATT_KV_W]).astype(BF16)
        att_ref[:, ATT_KV_W:] = acc[:, ATT_KV_W:].astype(BF16)

    @pl.when(n == ATT_TILES)
    def _():
        qk_ref[...] = proj() * (GLA_DK ** -0.5)

    @pl.when((n > ATT_TILES) & (n < QK_END))
    def _():
        qk_ref[...] = proj()

    @pl.when((n >= QK_END) & (n < VR_END))
    def _():
        vr_ref[...] = proj().astype(BF16)

    @pl.when(n >= VR_END)
    def _():
        gate_ref[...] = proj().astype(BF16)


def _inproj(x, mod6, g_mix, w_main, w_low, g_q, g_k, cos, sin, ones_bd, rot_bd):
    B, T, D = x.shape
    tm = min(TILES["inproj"], T)
    n_tiles = w_main.shape[1] // IN_TN
    gate_tiles = n_tiles - VR_END
    row = lambda b, i, n: (b, i, 0)
    vec = lambda b, i, n: (0, 0)
    tab = lambda b, i, n: (i, 0)

    def x_map(b, i, n):
        return (*_tile_ahead(b, i, jnp.where(n > 0, 1, 0), B, T // tm), 0)

    return pl.pallas_call(
        _inproj_kernel,
        grid=(B, T // tm, n_tiles),
        in_specs=[
            pl.BlockSpec((None, tm, D), x_map),
            pl.BlockSpec((None, 1, D), lambda b, i, n: (b * 6 + 1, 0, 0)),
            pl.BlockSpec((None, 1, D), lambda b, i, n: (b * 6 + 0, 0, 0)),
            pl.BlockSpec((1, D), vec),
            pl.BlockSpec((D, IN_TN), lambda b, i, n: (0, n)),
            pl.BlockSpec((D, LANES), vec),
            pl.BlockSpec((1, HEAD_DIM), vec),
            pl.BlockSpec((1, HEAD_DIM), vec),
            pl.BlockSpec((tm, HEAD_DIM), tab),
            pl.BlockSpec((tm, HEAD_DIM), tab),
            pl.BlockSpec((IN_TN, IN_TN), vec),
            pl.BlockSpec((IN_TN, IN_TN), vec),
        ],
        out_specs=[
            pl.BlockSpec((None, tm, IN_TN), lambda b, i, n: (b, i, jnp.minimum(n, ATT_TILES - 1))),
            pl.BlockSpec((None, tm, IN_TN), lambda b, i, n: (b, i, jnp.clip(n - ATT_TILES, 0, QK_TILES - 1))),
            pl.BlockSpec((None, tm, IN_TN), lambda b, i, n: (b, i, jnp.clip(n - QK_END, 0, VR_TILES - 1))),
            pl.BlockSpec((None, tm, IN_TN), lambda b, i, n: (b, i, jnp.clip(n - VR_END, 0, gate_tiles - 1))),
            pl.BlockSpec((None, tm, LANES), row),
        ],
        out_shape=[
            jax.ShapeDtypeStruct((B, T, ATT_COLS), BF16),
            jax.ShapeDtypeStruct((B, T, QK_COLS), F32),
            jax.ShapeDtypeStruct((B, T, VR_COLS), BF16),
            jax.ShapeDtypeStruct((B, T, gate_tiles * IN_TN), BF16),
            jax.ShapeDtypeStruct((B, T, LANES), F32),
        ],
        scratch_shapes=[pltpu.VMEM((tm, D), BF16)],
        compiler_params=_params(("arbitrary", "arbitrary", "arbitrary")),
        name="inproj",
    )(x, mod6, mod6, g_mix, w_main, w_low, g_q, g_k, cos, sin, ones_bd, rot_bd)


def _attn_kernel(q_ref, k_ref, v_ref, o_ref, v_ext):
    @pl.when(pl.program_id(2) == 0)
    def _():
        v_ext[:, :HEAD_DIM] = v_ref[...]
        v_ext[:, HEAD_DIM:] = jnp.ones((v_ext.shape[0], HEAD_DIM), BF16)

    k = k_ref[...]
    v = v_ext[...]
    for g in range(ATT_GROUP):
        cs = slice(g * HEAD_DIM, (g + 1) * HEAD_DIM)
        for r0 in range(0, q_ref.shape[0], ATT_UNIT_ROWS):
            rows = slice(r0, r0 + ATT_UNIT_ROWS)
            s = lax.dot_general(q_ref[rows, cs], k, NT, preferred_element_type=F32)
            m = jnp.max(s, axis=-1, keepdims=True)
            p = jnp.exp2((s - m).astype(BF16))
            o = jnp.dot(p, v, preferred_element_type=F32)
            o_ref[rows, cs] = (o[:, :HEAD_DIM] / o[:, HEAD_DIM:]).astype(BF16)


def _attention(qkv):
    B, T, _ = qkv.shape
    tq = min(TILES["attn"], T)
    gw = ATT_GROUP * HEAD_DIM
    k0 = ATT_W // HEAD_DIM
    v0 = (ATT_W + ATT_KV_W) // HEAD_DIM
    return pl.pallas_call(
        _attn_kernel,
        grid=(B, ATT_KV_HEADS, T // tq),
        in_specs=[
            pl.BlockSpec((None, tq, gw), lambda b, kv, i: (b, i, kv)),
            pl.BlockSpec((None, T, HEAD_DIM), lambda b, kv, i: (b, 0, k0 + kv)),
            pl.BlockSpec((None, T, HEAD_DIM), lambda b, kv, i: (b, 0, v0 + kv)),
        ],
        out_specs=pl.BlockSpec((None, tq, gw), lambda b, kv, i: (b, i, kv)),
        out_shape=jax.ShapeDtypeStruct((B, T, ATT_W), BF16),
        scratch_shapes=[pltpu.VMEM((T, 2 * HEAD_DIM), BF16)],
        compiler_params=_params(("arbitrary", "arbitrary", "arbitrary")),
        name="attn",
    )(qkv, qkv, qkv)


def _log_sigmoid(z):
    return jnp.minimum(z, 0.0) - jnp.log(1.0 + jnp.exp(-jnp.abs(z)))


def _gla_scan(q_ref, k_ref, v_ref, low_ref, w_ref, b_ref, o_ref, s_scr, d, forward, sub):
    C = GLA_CHUNK
    R = sub * C
    t_i = lax.broadcasted_iota(jnp.int32, (R, R), 0)
    s_i = lax.broadcasted_iota(jnp.int32, (R, R), 1)
    same = (t_i // C) == (s_i // C)
    if forward:
        cum = same & (s_i <= t_i)
        keep = cum
    else:
        cum = same & (s_i >= t_i)
        keep = same & (s_i > t_i)
    cum = jnp.where(cum, 1.0, 0.0).astype(BF16)

    z = jnp.dot(low_ref[...].astype(BF16), w_ref[...], preferred_element_type=F32) + b_ref[...]
    la = _log_sigmoid(z) * (LOG2_E / GATE_NORM)
    hi = la.astype(BF16)
    r1 = la - hi.astype(F32)
    mid = r1.astype(BF16)
    lo = (r1 - mid.astype(F32)).astype(BF16)
    bc = (jnp.dot(cum, hi, preferred_element_type=F32) + jnp.dot(cum, mid, preferred_element_type=F32)
          + jnp.dot(cum, lo, preferred_element_type=F32))
    e_pos = jnp.exp2(bc)
    e_neg = jnp.exp2(-bc)
    order = range(sub) if forward else reversed(range(sub))
    chunks = []
    for c in order:
        bc_c = bc[c * C:(c + 1) * C, :]
        btot = bc_c[C - 1:C, :] if forward else bc_c[0:1, :]
        chunks.append((c, jnp.exp2(btot - bc_c), jnp.exp2(btot)))

    for h in range(GLA_HEADS):
        ks = slice(h * GLA_DK, (h + 1) * GLA_DK)
        vs = slice(h * GLA_DV, (h + 1) * GLA_DV)
        k = k_ref[:, ks]
        v = v_ref[:, vs]
        qe = (q_ref[:, ks] * e_pos[:, ks]).astype(BF16)
        ke = (k * e_neg[:, ks]).astype(BF16)
        a = lax.dot_general(qe, ke, NT, preferred_element_type=F32)
        a = jnp.where(keep, a, 0.0).astype(BF16)
        o_intra = jnp.dot(a, v, preferred_element_type=F32)
        st = s_scr[d, h]
        for c, e_dec, e_tot in chunks:
            rows = slice(c * C, (c + 1) * C)
            o_ref[rows, vs] = o_intra[rows] + lax.dot_general(qe[rows], st.astype(BF16), NT,
                                                              preferred_element_type=F32)
            kd = (k[rows] * e_dec[:, ks]).astype(BF16)
            st = st * e_tot[:, ks] + lax.dot_general(v[rows], kd, TN, preferred_element_type=F32)
        s_scr[d, h] = st


def _gla_kernel(qf, kf, vf, lf, qb, kb, vb, lb, wf, wb, bf_, bb_, of, ob, s_scr, *, sub):
    @pl.when(pl.program_id(1) == 0)
    def _():
        s_scr[...] = jnp.zeros_like(s_scr)

    _gla_scan(qf, kf, vf, lf, wf, bf_, of, s_scr, 0, True, sub)
    _gla_scan(qb, kb, vb, lb, wb, bb_, ob, s_scr, 1, False, sub)


def _gla(qk, vr, low, w_up_f, b_f, w_up_b, b_b):
    B, T, _ = qk.shape
    rb = min(TILES["gla_rows"], T)
    nb = T // rb
    fwd = lambda b, c: (b, c, 0)
    bwd = lambda b, c: (b, nb - 1 - c, 0)
    vec = lambda b, c: (0, 0)

    def streams(row_map):
        r = lambda col: (lambda b, c: (*row_map(b, c)[:2], col))
        return [pl.BlockSpec((None, rb, GLA_QK), r(0)),
                pl.BlockSpec((None, rb, GLA_QK), r(1)),
                pl.BlockSpec((None, rb, GLA_V), r(0)),
                pl.BlockSpec((None, rb, LANES), r(0))]

    return pl.pallas_call(
        functools.partial(_gla_kernel, sub=rb // GLA_CHUNK),
        grid=(B, nb),
        in_specs=streams(fwd) + streams(bwd) + [
            pl.BlockSpec((LANES, GLA_QK), vec), pl.BlockSpec((LANES, GLA_QK), vec),
            pl.BlockSpec((1, GLA_QK), vec), pl.BlockSpec((1, GLA_QK), vec)],
        out_specs=[pl.BlockSpec((None, rb, GLA_V), fwd), pl.BlockSpec((None, rb, GLA_V), bwd)],
        out_shape=[jax.ShapeDtypeStruct((B, T, GLA_V), F32)] * 2,
        scratch_shapes=[pltpu.VMEM((2, GLA_HEADS, GLA_DV, GLA_DK), F32)],
        compiler_params=_params(("arbitrary", "arbitrary")),
        name="gla",
    )(qk, qk, vr, low, qk, qk, vr, low, w_up_f, w_up_b, b_f, b_b)


def _merge_kernel(att_ref, of_ref, ob_ref, r_ref, ga_ref, gg_ref, x_ref, gt_ref, sc_ref, sh_ref, ggla_ref, gffn_ref,
                  wa_ref, wg_ref, wo_ref, x1_ref, h2_ref):
    parts = []
    for h in range(GLA_HEADS):
        vs = slice(h * GLA_DV, (h + 1) * GLA_DV)
        o = of_ref[:, vs] + ob_ref[:, vs]
        ms = jnp.mean(o * o, axis=-1, keepdims=True)
        y = o * lax.rsqrt(ms + EPS) * ggla_ref[:, vs]
        parts.append((y * _silu(r_ref[:, vs].astype(F32))).astype(BF16))
    gl = jnp.concatenate(parts, axis=-1)
    a = jnp.dot(att_ref[...], wa_ref[...], preferred_element_type=F32)
    g = jnp.dot(gl, wg_ref[...], preferred_element_type=F32)
    gate_a = jax.nn.sigmoid(ga_ref[...].astype(F32))
    gate_g = jax.nn.sigmoid(gg_ref[...].astype(F32))
    merged = (gate_a * a + gate_g * g).astype(BF16)
    out = jnp.dot(merged, wo_ref[...], preferred_element_type=F32)
    x1 = x_ref[...] + gt_ref[...] * out
    x1_ref[...] = x1
    ms = jnp.mean(x1 * x1, axis=-1, keepdims=True)
    h2 = x1 * lax.rsqrt(ms + EPS) * gffn_ref[...]
    h2_ref[...] = (h2 * (1.0 + sc_ref[...]) + sh_ref[...]).astype(BF16)


def _merge(att, o_f, o_b, vr, gates, x, mod6, g_gla, g_ffn, w_br_att, w_br_gla, w_out):
    B, T, D = x.shape
    tm = min(TILES["merge"], T)
    row = lambda b, i: (b, i, 0)
    vec = lambda b, i: (0, 0)
    resident = lambda shape: pl.BlockSpec(shape, vec, pipeline_mode=pl.Buffered(1))
    return pl.pallas_call(
        _merge_kernel,
        grid=(B, T // tm),
        in_specs=[
            pl.BlockSpec((None, tm, ATT_W), row),
            pl.BlockSpec((None, tm, GLA_V), row),
            pl.BlockSpec((None, tm, GLA_V), row),
            pl.BlockSpec((None, tm, GLA_V), lambda b, i: (b, i, 1)),
            pl.BlockSpec((None, tm, D), lambda b, i: (b, i, 0)),
            pl.BlockSpec((None, tm, D), lambda b, i: (b, i, 1)),
            pl.BlockSpec((None, tm, D), row),
            pl.BlockSpec((None, 1, D), lambda b, i: (b * 6 + 2, 0, 0)),
            pl.BlockSpec((None, 1, D), lambda b, i: (b * 6 + 4, 0, 0)),
            pl.BlockSpec((None, 1, D), lambda b, i: (b * 6 + 3, 0, 0)),
            pl.BlockSpec((1, GLA_V), vec),
            pl.BlockSpec((1, D), vec),
            resident((ATT_W, D)),
            resident((GLA_V, D)),
            resident((D, D)),
        ],
        out_specs=[pl.BlockSpec((None, tm, D), row), pl.BlockSpec((None, tm, D), row)],
        out_shape=[jax.ShapeDtypeStruct((B, T, D), F32), jax.ShapeDtypeStruct((B, T, D), BF16)],
        compiler_params=_params(("arbitrary", "arbitrary")),
        name="merge",
    )(att, o_f, o_b, vr, gates, gates, x, mod6, mod6, mod6, g_gla, g_ffn, w_br_att, w_br_gla, w_out)


def _ffn_kernel(h_ref, hp_ref, hn_ref, wv_ref, wg_ref, cv_ref, cg_ref, bv_ref, bg_ref, wd_ref, x1_ref, gt_ref,
                gf_ref, perm_ref, y_ref, hext, acc, *, tm):
    i = pl.program_id(1)
    j = pl.program_id(2)
    halo = BF16_ROWS
    sub = 8
    G = tm // sub
    lane_tiles = acc.shape[0]

    def lanes(c):
        return slice(c * LANES, (c + 1) * LANES)

    def gather_rows(ref, start, stride):
        return jnp.concatenate([ref[c, pl.ds(start, sub, stride=stride), :] for c in range(lane_tiles)], axis=1)

    @pl.when(j == 0)
    def _():
        has_prev = (i > 0).astype(F32)
        has_next = (i < pl.num_programs(1) - 1).astype(F32)
        before = hp_ref[...].astype(F32)[halo - 1:halo] * has_prev
        after = hn_ref[...].astype(F32)[0:1] * has_next
        edge = jnp.concatenate([before, after, jnp.zeros((halo - 2, before.shape[1]), F32)], axis=0)
        hext[tm:, :] = edge.astype(BF16)
        hext[:tm, :] = jnp.dot(perm_ref[...], h_ref[...], preferred_element_type=F32).astype(BF16)
        acc[...] = jnp.zeros_like(acc)

    he = hext[...]
    row_id = lax.broadcasted_iota(jnp.int32, (sub, wv_ref.shape[1]), 0)

    def conv(u, c_ref, b_ref):
        main = u[:tm]
        before = u[tm:tm + 1]
        after = u[tm + 1:tm + 2]
        first = jnp.where(row_id == 0, before, pltpu.roll(main[tm - sub:], 1, axis=0))
        last = jnp.where(row_id == sub - 1, after, pltpu.roll(main[:sub], sub - 1, axis=0))
        prev = jnp.concatenate([first, main[:tm - sub]], axis=0)
        nxt = jnp.concatenate([main[sub:], last], axis=0)
        return c_ref[0:1, :] * prev + c_ref[1:2, :] * main + c_ref[2:3, :] * nxt + b_ref[...]

    val = conv(jnp.dot(he, wv_ref[...], preferred_element_type=F32), cv_ref, bv_ref)
    gate = conv(jnp.dot(he, wg_ref[...], preferred_element_type=F32), cg_ref, bg_ref)
    act = (_silu(gate) * val).astype(BF16)
    down = jnp.dot(act, wd_ref[...], preferred_element_type=F32)
    for c in range(lane_tiles):
        acc[c] += down[:, lanes(c)]

    @pl.when(j == pl.num_programs(2) - 1)
    def _():
        for g in range(G):
            rows = slice(g * sub, (g + 1) * sub)
            ffn = gather_rows(acc, (g % (G // sub)) * sub * sub + g // (G // sub), sub)
            x2 = x1_ref[rows, :] + gt_ref[...] * ffn
            ms = jnp.mean(x2 * x2, axis=-1, keepdims=True)
            y_ref[rows, :] = x2 * lax.rsqrt(ms + EPS) * gf_ref[...]


def _row_permutation(tm):
    pos = jnp.arange(tm)
    token = (pos % 8) * (tm // 8) + pos // 8
    return (token[:, None] == jnp.arange(tm)[None, :]).astype(BF16)


def _ffn(h2, x1, mod6, w_up, w_conv, b_conv, w_down, g_final):
    B, T, D = x1.shape
    d_ff = w_down.shape[0]
    tm = min(TILES["ffn"], T)
    tf = TILES["ffn_cols"]
    nj = d_ff // tf
    halo = BF16_ROWS
    row = lambda b, i, j: (b, i, 0)
    nt = T // tm

    def h_tile(b, i, j):
        return _tile_ahead(b, i, jnp.where(j > 0, 1, 0), B, nt)

    def h_map(b, i, j):
        return (*h_tile(b, i, j), 0)

    def hp_map(b, i, j):
        bb, ii = h_tile(b, i, j)
        return (bb, jnp.maximum(ii * (tm // halo) - 1, 0), 0)

    def hn_map(b, i, j):
        bb, ii = h_tile(b, i, j)
        return (bb, jnp.minimum((ii + 1) * (tm // halo), T // halo - 1), 0)

    def x1_map(b, i, j):
        return (*_tile_ahead(b, i, jnp.where(j == 0, -1, 0), B, nt), 0)

    return pl.pallas_call(
        functools.partial(_ffn_kernel, tm=tm),
        grid=(B, T // tm, nj),
        in_specs=[
            pl.BlockSpec((None, tm, D), h_map),
            pl.BlockSpec((None, halo, D), hp_map),
            pl.BlockSpec((None, halo, D), hn_map),
            pl.BlockSpec((D, tf), lambda b, i, j: (0, j)),
            pl.BlockSpec((D, tf), lambda b, i, j: (0, nj + j)),
            pl.BlockSpec((3, tf), lambda b, i, j: (0, j)),
            pl.BlockSpec((3, tf), lambda b, i, j: (0, nj + j)),
            pl.BlockSpec((1, tf), lambda b, i, j: (0, j)),
            pl.BlockSpec((1, tf), lambda b, i, j: (0, nj + j)),
            pl.BlockSpec((tf, D), lambda b, i, j: (j, 0)),
            pl.BlockSpec((None, tm, D), x1_map),
            pl.BlockSpec((None, 1, D), lambda b, i, j: (b * 6 + 5, 0, 0)),
            pl.BlockSpec((1, D), lambda b, i, j: (0, 0)),
            pl.BlockSpec((tm, tm), lambda b, i, j: (0, 0)),
        ],
        out_specs=pl.BlockSpec((None, tm, D), row),
        out_shape=jax.ShapeDtypeStruct((B, T, D), F32),
        scratch_shapes=[pltpu.VMEM((tm + halo, D), BF16),
                        pltpu.VMEM((D // LANES, tm, LANES), F32)],
        compiler_params=_params(("arbitrary", "arbitrary", "arbitrary")),
        name="ffn",
    )(h2, h2, h2, w_up, w_up, w_conv, w_conv, b_conv, b_conv, w_down, x1, mod6, g_final, _row_permutation(tm))


def _rope_tables(T):
    rows = T // GRID_W
    axis_dim = HEAD_DIM // 2
    row = jnp.repeat(jnp.arange(rows, dtype=F32), GRID_W)
    col = jnp.tile(jnp.arange(GRID_W, dtype=F32), rows)
    inv = ROPE_THETA ** (-jnp.arange(0, axis_dim, 2, dtype=F32) / axis_dim)
    ar = row[:, None] * inv
    ac = col[:, None] * inv
    ang = jnp.concatenate([ar, ar, ac, ac], axis=-1)
    return jnp.cos(ang), jnp.sin(ang)


def _rope_matrices():
    j = jnp.arange(IN_TN)
    src, dst = j[:, None], j[None, :]
    half = HEAD_DIM // 4
    lower = (dst % (2 * half)) < half
    rot = jnp.where(lower & (src == dst + half), -1.0, 0.0) + jnp.where(~lower & (src == dst - half), 1.0, 0.0)
    ones = jnp.where(src // HEAD_DIM == dst // HEAD_DIM, 1.0, 0.0)
    return ones.astype(BF16), rot.astype(BF16)


def _prep_weights(w_mod, b_mod, g_mix_norm, w_in, g_q, g_k, w_a_up_f, b_a_f, w_a_up_b, b_a_b, g_gla, w_br_att,
                  w_br_gla, w_out, g_ffn_norm, w_up, w_conv, b_conv, w_down, g_final):
    D = w_in.shape[1]
    low0 = ATT_COLS + QK_COLS + VR_COLS
    low1 = low0 + 2 * GLA_RANK
    wi = w_in[0]
    w_main = jnp.concatenate([wi[:, :low0], wi[:, low1:]], axis=1).astype(BF16)
    w_low = jnp.pad(wi[:, low0:low1], ((0, 0), (0, LANES - 2 * GLA_RANK))).astype(BF16)
    up_f = jnp.pad(w_a_up_f[0], ((0, LANES - GLA_RANK), (0, 0))).astype(BF16)
    up_b = jnp.pad(w_a_up_b[0], ((GLA_RANK, LANES - 2 * GLA_RANK), (0, 0))).astype(BF16)
    return dict(
        w_mod=w_mod[0].astype(BF16), b_mod=b_mod[0], g_mix=g_mix_norm[0].reshape(1, D),
        w_main=w_main, w_low=w_low, g_q=g_q[0].reshape(1, HEAD_DIM) * (HEAD_DIM ** -0.5 * LOG2_E),
        g_k=g_k[0].reshape(1, HEAD_DIM),
        up_f=up_f, up_b=up_b, b_f=b_a_f[0].reshape(1, GLA_QK), b_b=b_a_b[0].reshape(1, GLA_QK),
        g_gla=g_gla[0].reshape(1, GLA_V), w_br_att=w_br_att[0].astype(BF16), w_br_gla=w_br_gla[0].astype(BF16),
        w_out=w_out[0].astype(BF16), g_ffn=g_ffn_norm[0].reshape(1, D), w_up=w_up[0].astype(BF16),
        w_conv=w_conv[0], b_conv=b_conv[0].reshape(1, -1), w_down=w_down[0].astype(BF16),
        g_final=g_final.reshape(1, D))


def _trunk(x, mod, W):
    B, T, D = x.shape
    mod6 = mod.reshape(B * 6, 1, D)
    cos, sin = _rope_tables(T)
    ones_bd, rot_bd = _rope_matrices()
    qkv, qk, vr, gates, low = _inproj(x, mod6, W["g_mix"], W["w_main"], W["w_low"], W["g_q"], W["g_k"],
                                      cos, sin, ones_bd, rot_bd)
    att = _attention(qkv)
    o_f, o_b = _gla(qk, vr, low, W["up_f"], W["b_f"], W["up_b"], W["b_b"])
    x1, h2 = _merge(att, o_f, o_b, vr, gates, x, mod6, W["g_gla"], W["g_ffn"], W["w_br_att"], W["w_br_gla"],
                    W["w_out"])
    return _ffn(h2, x1, mod6, W["w_up"], W["w_conv"], W["b_conv"], W["w_down"], W["g_final"])


def kernel(x_prompt, x_sample, c_prompt, c_sample, w_mod, b_mod, g_mix_norm, w_in, g_q, g_k, w_a_up_f, b_a_f,
           w_a_up_b, b_a_b, g_gla, w_br_att, w_br_gla, w_out, g_ffn_norm, w_up, w_conv, b_conv, w_down, g_final):
    assert w_mod.shape[0] == 1, "single-layer trunk"
    W = _prep_weights(w_mod, b_mod, g_mix_norm, w_in, g_q, g_k, w_a_up_f, b_a_f, w_a_up_b, b_a_b, g_gla, w_br_att,
                      w_br_gla, w_out, g_ffn_norm, w_up, w_conv, b_conv, w_down, g_final)
    nb = x_prompt.shape[0]
    mod = _mod(jnp.concatenate([c_prompt, c_sample], axis=0), W["w_mod"], W["b_mod"])
    return _trunk(x_prompt, mod[:nb], W), _trunk(x_sample, mod[nb:], W)
```

```python
import functools

import jax
import jax.numpy as jnp
from jax import lax
from jax.experimental import pallas as pl
from jax.experimental.pallas import tpu as pltpu

F32 = jnp.float32
BF16 = jnp.bfloat16
EPS = 1e-6
LOG2_E = 1.4426950408889634

HEAD_DIM = 128
ATT_HEADS = 8
ATT_KV_HEADS = 2
ATT_GROUP = ATT_HEADS // ATT_KV_HEADS
ATT_W = ATT_HEADS * HEAD_DIM
ATT_KV_W = ATT_KV_HEADS * HEAD_DIM
GRID_W = 64
ROPE_THETA = 10000.0
GLA_HEADS = 4
GLA_DK = 128
GLA_DV = 256
GLA_QK = GLA_HEADS * GLA_DK
GLA_V = GLA_HEADS * GLA_DV
GLA_RANK = 16
GATE_NORM = 16.0
GLA_CHUNK = 64
LANES = 128
BF16_ROWS = 16
VMEM_LIMIT = 56 * 1024 * 1024

IN_TN = 512
ATT_COLS = ATT_W + 2 * ATT_KV_W
QK_COLS = 2 * GLA_QK
VR_COLS = 2 * GLA_V
ATT_TILES = ATT_COLS // IN_TN
QK_TILES = QK_COLS // IN_TN
VR_TILES = VR_COLS // IN_TN
QK_END = ATT_TILES + QK_TILES
VR_END = QK_END + VR_TILES

TILES = dict(inproj=1024, attn=512, gla_rows=256, merge=256, ffn=512, ffn_cols=512)
ATT_UNIT_ROWS = 128
FFN_UNIT_ROWS = 128

NT = (((1,), (1,)), ((), ()))
TN = (((0,), (0,)), ((), ()))


def _params(sem):
    return pltpu.CompilerParams(dimension_semantics=sem, vmem_limit_bytes=VMEM_LIMIT)


def _silu(v):
    return v * jax.nn.sigmoid(v)


def _tile_ahead(b, i, ahead, n_batch, n_tiles):
    lin = jnp.clip(b * n_tiles + i + ahead, 0, n_batch * n_tiles - 1)
    return lin // n_tiles, lin % n_tiles


def _mod_kernel(c_ref, w_ref, b_ref, o_ref):
    s = _silu(c_ref[...]).astype(BF16)
    o_ref[...] = jnp.dot(s, w_ref[...], preferred_element_type=F32) + b_ref[...]


def _mod(c, w_mod, b_mod):
    n, d = c.shape
    cols = w_mod.shape[1]
    tn = 2048
    return pl.pallas_call(
        _mod_kernel,
        grid=(cols // tn,),
        in_specs=[pl.BlockSpec((n, d), lambda j: (0, 0)),
                  pl.BlockSpec((d, tn), lambda j: (0, j)),
                  pl.BlockSpec((1, tn), lambda j: (0, j))],
        out_specs=pl.BlockSpec((n, tn), lambda j: (0, j)),
        out_shape=jax.ShapeDtypeStruct((n, cols), F32),
        compiler_params=_params(("arbitrary",)),
        name="mod",
    )(c, w_mod, b_mod.reshape(1, cols))


def _norm_rope(a, g, cos, sin, ones_bd, rot_bd):
    heads = a.shape[1] // HEAD_DIM
    ssq = jnp.dot((a * a).astype(BF16), ones_bd, preferred_element_type=F32)
    y = a * lax.rsqrt(ssq * (1.0 / HEAD_DIM) + EPS) * jnp.concatenate([g] * heads, axis=1)
    rot = jnp.dot(y.astype(BF16), rot_bd, preferred_element_type=F32)
    return y * jnp.concatenate([cos] * heads, axis=1) + rot * jnp.concatenate([sin] * heads, axis=1)


def _inproj_kernel(x_ref, sc_ref, sh_ref, g_ref, w_ref, wlow_ref, gq_ref, gk_ref, cos_ref, sin_ref, ones_ref, rot_ref,
                   att_ref, qk_ref, vr_ref, gate_ref, low_ref, h_scr):
    n = pl.program_id(2)

    @pl.when(n == 0)
    def _():
        x = x_ref[...]
        ms = jnp.mean(x * x, axis=-1, keepdims=True)
        h = x * lax.rsqrt(ms + EPS) * g_ref[...]
        hb = (h * (1.0 + sc_ref[...]) + sh_ref[...]).astype(BF16)
        h_scr[...] = hb
        low_ref[...] = jnp.dot(hb, wlow_ref[...], preferred_element_type=F32)

    def proj():
        return jnp.dot(h_scr[...], w_ref[...], preferred_element_type=F32)

    @pl.when(n < ATT_W // IN_TN)
    def _():
        att_ref[...] = _norm_rope(proj(), gq_ref[...], cos_ref[...], sin_ref[...], ones_ref[...],
                                  rot_ref[...]).astype(BF16)

    @pl.when(n == ATT_W // IN_TN)
    def _():
        acc = proj()
        att_ref[:, :ATT_KV_W] = _norm_rope(acc[:, :ATT_KV_W], gk_ref[...], cos_ref[...], sin_ref[...],
                                           ones_ref[:ATT_KV_W, :ATT_KV_W], rot_ref[:ATT_KV_W, :ATT_KV_W]).astype(BF16)
        att_ref[:, ATT_KV_W:] = acc[:, ATT_KV_W:].astype(BF16)

    @pl.when(n == ATT_TILES)
    def _():
        qk_ref[...] = proj() * (GLA_DK ** -0.5)

    @pl.when((n > ATT_TILES) & (n < QK_END))
    def _():
        qk_ref[...] = proj()

    @pl.when((n >= QK_END) & (n < VR_END))
    def _():
        vr_ref[...] = proj().astype(BF16)

    @pl.when(n >= VR_END)
    def _():
        gate_ref[...] = proj().astype(BF16)


def _inproj(x, mod6, g_mix, w_main, w_low, g_q, g_k, cos, sin, ones_bd, rot_bd):
    B, T, D = x.shape
    tm = min(TILES["inproj"], T)
    n_tiles = w_main.shape[1] // IN_TN
    gate_tiles = n_tiles - VR_END
    row = lambda b, i, n: (b, i, 0)
    vec = lambda b, i, n: (0, 0)
    tab = lambda b, i, n: (i, 0)

    def x_map(b, i, n):
        return (*_tile_ahead(b, i, jnp.where(n > 0, 1, 0), B, T // tm), 0)

    return pl.pallas_call(
        _inproj_kernel,
        grid=(B, T // tm, n_tiles),
        in_specs=[
            pl.BlockSpec((None, tm, D), x_map),
            pl.BlockSpec((None, 1, D), lambda b, i, n: (b * 6 + 1, 0, 0)),
            pl.BlockSpec((None, 1, D), lambda b, i, n: (b * 6 + 0, 0, 0)),
            pl.BlockSpec((1, D), vec),
            pl.BlockSpec((D, IN_TN), lambda b, i, n: (0, n)),
            pl.BlockSpec((D, LANES), vec),
            pl.BlockSpec((1, HEAD_DIM), vec),
            pl.BlockSpec((1, HEAD_DIM), vec),
            pl.BlockSpec((tm, HEAD_DIM), tab),
            pl.BlockSpec((tm, HEAD_DIM), tab),
            pl.BlockSpec((IN_TN, IN_TN), vec),
            pl.BlockSpec((IN_TN, IN_TN), vec),
        ],
        out_specs=[
            pl.BlockSpec((None, tm, IN_TN), lambda b, i, n: (b, i, jnp.minimum(n, ATT_TILES - 1))),
            pl.BlockSpec((None, tm, IN_TN), lambda b, i, n: (b, i, jnp.clip(n - ATT_TILES, 0, QK_TILES - 1))),
            pl.BlockSpec((None, tm, IN_TN), lambda b, i, n: (b, i, jnp.clip(n - QK_END, 0, VR_TILES - 1))),
            pl.BlockSpec((None, tm, IN_TN), lambda b, i, n: (b, i, jnp.clip(n - VR_END, 0, gate_tiles - 1))),
            pl.BlockSpec((None, tm, LANES), row),
        ],
        out_shape=[
            jax.ShapeDtypeStruct((B, T, ATT_COLS), BF16),
            jax.ShapeDtypeStruct((B, T, QK_COLS), F32),
            jax.ShapeDtypeStruct((B, T, VR_COLS), BF16),
            jax.ShapeDtypeStruct((B, T, gate_tiles * IN_TN), BF16),
            jax.ShapeDtypeStruct((B, T, LANES), F32),
        ],
        scratch_shapes=[pltpu.VMEM((tm, D), BF16)],
        compiler_params=_params(("arbitrary", "arbitrary", "arbitrary")),
        name="inproj",
    )(x, mod6, mod6, g_mix, w_main, w_low, g_q, g_k, cos, sin, ones_bd, rot_bd)


def _attn_kernel(q_ref, k_ref, v_ref, o_ref, v_ext):
    @pl.when(pl.program_id(2) == 0)
    def _():
        v_ext[:, :HEAD_DIM] = v_ref[...]
        v_ext[:, HEAD_DIM:] = jnp.ones((v_ext.shape[0], HEAD_DIM), BF16)

    k = k_ref[...]
    v = v_ext[...]
    for g in range(ATT_GROUP):
        cs = slice(g * HEAD_DIM, (g + 1) * HEAD_DIM)
        for r0 in range(0, q_ref.shape[0], ATT_UNIT_ROWS):
            rows = slice(r0, r0 + ATT_UNIT_ROWS)
            s = lax.dot_general(q_ref[rows, cs], k, NT, preferred_element_type=F32)
            m = jnp.max(s, axis=-1, keepdims=True)
            p = jnp.exp2((s - m).astype(BF16))
            o = jnp.dot(p, v, preferred_element_type=F32)
            o_ref[rows, cs] = (o[:, :HEAD_DIM] / o[:, HEAD_DIM:]).astype(BF16)


def _attention(qkv):
    B, T, _ = qkv.shape
    tq = min(TILES["attn"], T)
    gw = ATT_GROUP * HEAD_DIM
    k0 = ATT_W // HEAD_DIM
    v0 = (ATT_W + ATT_KV_W) // HEAD_DIM
    return pl.pallas_call(
        _attn_kernel,
        grid=(B, ATT_KV_HEADS, T // tq),
        in_specs=[
            pl.BlockSpec((None, tq, gw), lambda b, kv, i: (b, i, kv)),
            pl.BlockSpec((None, T, HEAD_DIM), lambda b, kv, i: (b, 0, k0 + kv)),
            pl.BlockSpec((None, T, HEAD_DIM), lambda b, kv, i: (b, 0, v0 + kv)),
        ],
        out_specs=pl.BlockSpec((None, tq, gw), lambda b, kv, i: (b, i, kv)),
        out_shape=jax.ShapeDtypeStruct((B, T, ATT_W), BF16),
        scratch_shapes=[pltpu.VMEM((T, 2 * HEAD_DIM), BF16)],
        compiler_params=_params(("arbitrary", "arbitrary", "arbitrary")),
        name="attn",
    )(qkv, qkv, qkv)


def _log_sigmoid(z):
    return jnp.minimum(z, 0.0) - jnp.log(1.0 + jnp.exp(-jnp.abs(z)))


def _gla_scan(q_ref, k_ref, v_ref, low_ref, w_ref, b_ref, o_ref, s_scr, d, forward, sub):
    C = GLA_CHUNK
    R = sub * C
    t_i = lax.broadcasted_iota(jnp.int32, (R, R), 0)
    s_i = lax.broadcasted_iota(jnp.int32, (R, R), 1)
    same = (t_i // C) == (s_i // C)
    if forward:
        cum = same & (s_i <= t_i)
        keep = cum
    else:
        cum = same & (s_i >= t_i)
        keep = same & (s_i > t_i)
    cum = jnp.where(cum, 1.0, 0.0).astype(BF16)

    z = jnp.dot(low_ref[...].astype(BF16), w_ref[...], preferred_element_type=F32) + b_ref[...]
    la = _log_sigmoid(z) * (LOG2_E / GATE_NORM)
    hi = la.astype(BF16)
    r1 = la - hi.astype(F32)
    mid = r1.astype(BF16)
    lo = (r1 - mid.astype(F32)).astype(BF16)
    bc = (jnp.dot(cum, hi, preferred_element_type=F32) + jnp.dot(cum, mid, preferred_element_type=F32)
          + jnp.dot(cum, lo, preferred_element_type=F32))
    e_pos = jnp.exp2(bc)
    e_neg = jnp.exp2(-bc)
    order = range(sub) if forward else reversed(range(sub))
    chunks = []
    for c in order:
        bc_c = bc[c * C:(c + 1) * C, :]
        btot = bc_c[C - 1:C, :] if forward else bc_c[0:1, :]
        chunks.append((c, jnp.exp2(btot - bc_c), jnp.exp2(btot)))

    for h in range(GLA_HEADS):
        ks = slice(h * GLA_DK, (h + 1) * GLA_DK)
        vs = slice(h * GLA_DV, (h + 1) * GLA_DV)
        k = k_ref[:, ks]
        v = v_ref[:, vs]
        qe = (q_ref[:, ks] * e_pos[:, ks]).astype(BF16)
        ke = (k * e_neg[:, ks]).astype(BF16)
        a = lax.dot_general(qe, ke, NT, preferred_element_type=F32)
        a = jnp.where(keep, a, 0.0).astype(BF16)
        o_intra = jnp.dot(a, v, preferred_element_type=F32)
        st = s_scr[d, h]
        for c, e_dec, e_tot in chunks:
            rows = slice(c * C, (c + 1) * C)
            o_ref[rows, vs] = o_intra[rows] + lax.dot_general(qe[rows], st.astype(BF16), NT,
                                                              preferred_element_type=F32)
            kd = (k[rows] * e_dec[:, ks]).astype(BF16)
            st = st * e_tot[:, ks] + lax.dot_general(v[rows], kd, TN, preferred_element_type=F32)
        s_scr[d, h] = st


def _gla_kernel(qf, kf, vf, lf, qb, kb, vb, lb, wf, wb, bf_, bb_, of, ob, s_scr, *, sub):
    @pl.when(pl.program_id(1) == 0)
    def _():
        s_scr[...] = jnp.zeros_like(s_scr)

    _gla_scan(qf, kf, vf, lf, wf, bf_, of, s_scr, 0, True, sub)
    _gla_scan(qb, kb, vb, lb, wb, bb_, ob, s_scr, 1, False, sub)


def _gla(qk, vr, low, w_up_f, b_f, w_up_b, b_b):
    B, T, _ = qk.shape
    rb = min(TILES["gla_rows"], T)
    nb = T // rb
    fwd = lambda b, c: (b, c, 0)
    bwd = lambda b, c: (b, nb - 1 - c, 0)
    vec = lambda b, c: (0, 0)

    def streams(row_map):
        r = lambda col: (lambda b, c: (*row_map(b, c)[:2], col))
        return [pl.BlockSpec((None, rb, GLA_QK), r(0)),
                pl.BlockSpec((None, rb, GLA_QK), r(1)),
                pl.BlockSpec((None, rb, GLA_V), r(0)),
                pl.BlockSpec((None, rb, LANES), r(0))]

    return pl.pallas_call(
        functools.partial(_gla_kernel, sub=rb // GLA_CHUNK),
        grid=(B, nb),
        in_specs=streams(fwd) + streams(bwd) + [
            pl.BlockSpec((LANES, GLA_QK), vec), pl.BlockSpec((LANES, GLA_QK), vec),
            pl.BlockSpec((1, GLA_QK), vec), pl.BlockSpec((1, GLA_QK), vec)],
        out_specs=[pl.BlockSpec((None, rb, GLA_V), fwd), pl.BlockSpec((None, rb, GLA_V), bwd)],
        out_shape=[jax.ShapeDtypeStruct((B, T, GLA_V), F32)] * 2,
        scratch_shapes=[pltpu.VMEM((2, GLA_HEADS, GLA_DV, GLA_DK), F32)],
        compiler_params=_params(("arbitrary", "arbitrary")),
        name="gla",
    )(qk, qk, vr, low, qk, qk, vr, low, w_up_f, w_up_b, b_f, b_b)


def _merge_kernel(att_ref, of_ref, ob_ref, r_ref, ga_ref, gg_ref, x_ref, gt_ref, sc_ref, sh_ref, ggla_ref, gffn_ref,
                  wa_ref, wg_ref, wo_ref, x1_ref, h2_ref):
    parts = []
    for h in range(GLA_HEADS):
        vs = slice(h * GLA_DV, (h + 1) * GLA_DV)
        o = of_ref[:, vs] + ob_ref[:, vs]
        ms = jnp.mean(o * o, axis=-1, keepdims=True)
        y = o * lax.rsqrt(ms + EPS) * ggla_ref[:, vs]
        parts.append((y * _silu(r_ref[:, vs].astype(F32))).astype(BF16))
    gl = jnp.concatenate(parts, axis=-1)
    a = jnp.dot(att_ref[...], wa_ref[...], preferred_element_type=F32)
    g = jnp.dot(gl, wg_ref[...], preferred_element_type=F32)
    gate_a = jax.nn.sigmoid(ga_ref[...].astype(F32))
    gate_g = jax.nn.sigmoid(gg_ref[...].astype(F32))
    merged = (gate_a * a + gate_g * g).astype(BF16)
    out = jnp.dot(merged, wo_ref[...], preferred_element_type=F32)
    x1 = x_ref[...] + gt_ref[...] * out
    x1_ref[...] = x1
    ms = jnp.mean(x1 * x1, axis=-1, keepdims=True)
    h2 = x1 * lax.rsqrt(ms + EPS) * gffn_ref[...]
    h2_ref[...] = (h2 * (1.0 + sc_ref[...]) + sh_ref[...]).astype(BF16)


def _merge(att, o_f, o_b, vr, gates, x, mod6, g_gla, g_ffn, w_br_att, w_br_gla, w_out):
    B, T, D = x.shape
    tm = min(TILES["merge"], T)
    row = lambda b, i: (b, i, 0)
    vec = lambda b, i: (0, 0)
    resident = lambda shape: pl.BlockSpec(shape, vec, pipeline_mode=pl.Buffered(1))
    return pl.pallas_call(
        _merge_kernel,
        grid=(B, T // tm),
        in_specs=[
            pl.BlockSpec((None, tm, ATT_W), row),
            pl.BlockSpec((None, tm, GLA_V), row),
            pl.BlockSpec((None, tm, GLA_V), row),
            pl.BlockSpec((None, tm, GLA_V), lambda b, i: (b, i, 1)),
            pl.BlockSpec((None, tm, D), lambda b, i: (b, i, 0)),
            pl.BlockSpec((None, tm, D), lambda b, i: (b, i, 1)),
            pl.BlockSpec((None, tm, D), row),
            pl.BlockSpec((None, 1, D), lambda b, i: (b * 6 + 2, 0, 0)),
            pl.BlockSpec((None, 1, D), lambda b, i: (b * 6 + 4, 0, 0)),
            pl.BlockSpec((None, 1, D), lambda b, i: (b * 6 + 3, 0, 0)),
            pl.BlockSpec((1, GLA_V), vec),
            pl.BlockSpec((1, D), vec),
            resident((ATT_W, D)),
            resident((GLA_V, D)),
            resident((D, D)),
        ],
        out_specs=[pl.BlockSpec((None, tm, D), row), pl.BlockSpec((None, tm, D), row)],
        out_shape=[jax.ShapeDtypeStruct((B, T, D), F32), jax.ShapeDtypeStruct((B, T, D), BF16)],
        compiler_params=_params(("arbitrary", "arbitrary")),
        name="merge",
    )(att, o_f, o_b, vr, gates, gates, x, mod6, mod6, mod6, g_gla, g_ffn, w_br_att, w_br_gla, w_out)


def _ffn_kernel(h_ref, hp_ref, hn_ref, wv_ref, wg_ref, cv_ref, cg_ref, bv_ref, bg_ref, wd_ref, x1_ref, gt_ref,
                gf_ref, perm_ref, y_ref, hext, u_scr, acc, *, tm):
    i = pl.program_id(1)
    j = pl.program_id(2)
    halo = BF16_ROWS
    sub = 8
    G = tm // sub
    lane_tiles = acc.shape[0]

    def lanes(c):
        return slice(c * LANES, (c + 1) * LANES)

    def gather_rows(ref, start, stride):
        return jnp.concatenate([ref[c, pl.ds(start, sub, stride=stride), :] for c in range(lane_tiles)], axis=1)

    @pl.when(j == 0)
    def _():
        has_prev = (i > 0).astype(F32)
        has_next = (i < pl.num_programs(1) - 1).astype(F32)
        before = hp_ref[...].astype(F32)[halo - 1:halo] * has_prev
        after = hn_ref[...].astype(F32)[0:1] * has_next
        edge = jnp.concatenate([before, after, jnp.zeros((halo - 2, before.shape[1]), F32)], axis=0)
        hext[tm:, :] = edge.astype(BF16)
        hext[:tm, :] = jnp.dot(perm_ref[...], h_ref[...], preferred_element_type=F32).astype(BF16)
        acc[...] = jnp.zeros_like(acc)

    row_id = lax.broadcasted_iota(jnp.int32, (sub, wv_ref.shape[1]), 0)
    unit = min(FFN_UNIT_ROWS, tm)
    n_units = tm // unit

    def up(q):
        r0, r1 = q * unit, (q + 1) * unit + (halo if q == n_units - 1 else 0)
        lhs = hext[r0:r1, :]
        u_scr[0, r0:r1, :] = jnp.dot(lhs, wv_ref[...], preferred_element_type=F32)
        u_scr[1, r0:r1, :] = jnp.dot(lhs, wg_ref[...], preferred_element_type=F32)

    def conv(q, k, c_ref, b_ref):
        r0, r1 = q * unit, (q + 1) * unit
        main = u_scr[k, r0:r1, :]
        if q > 0:
            prev = u_scr[k, r0 - sub:r1 - sub, :]
        else:
            before = u_scr[k, tm:tm + 1, :]
            first = jnp.where(row_id == 0, before, pltpu.roll(u_scr[k, tm - sub:tm, :], 1, axis=0))
            prev = jnp.concatenate([first, u_scr[k, 0:r1 - sub, :]], axis=0)
        if q < n_units - 1:
            nxt = u_scr[k, r0 + sub:r1 + sub, :]
        else:
            after = u_scr[k, tm + 1:tm + 2, :]
            last = jnp.where(row_id == sub - 1, after, pltpu.roll(u_scr[k, 0:sub, :], sub - 1, axis=0))
            nxt = jnp.concatenate([u_scr[k, r0 + sub:tm, :], last], axis=0)
        return c_ref[0:1, :] * prev + c_ref[1:2, :] * main + c_ref[2:3, :] * nxt + b_ref[...]

    def gate_down(q):
        r0, r1 = q * unit, (q + 1) * unit
        act = (_silu(conv(q, 1, cg_ref, bg_ref)) * conv(q, 0, cv_ref, bv_ref)).astype(BF16)
        down = jnp.dot(act, wd_ref[...], preferred_element_type=F32)
        for c in range(lane_tiles):
            acc[c, r0:r1, :] += down[:, lanes(c)]

    issued = set()

    def ensure_up(q):
        if q not in issued:
            issued.add(q)
            up(q)

    ensure_up(n_units - 1)
    for q in range(n_units):
        for ahead in range(q, min(q + 3, n_units)):
            ensure_up(ahead)
        gate_down(q)

    @pl.when(j == pl.num_programs(2) - 1)
    def _():
        for g in range(G):
            rows = slice(g * sub, (g + 1) * sub)
            ffn = gather_rows(acc, (g % (G // sub)) * sub * sub + g // (G // sub), sub)
            x2 = x1_ref[rows, :] + gt_ref[...] * ffn
            ms = jnp.mean(x2 * x2, axis=-1, keepdims=True)
            y_ref[rows, :] = x2 * lax.rsqrt(ms + EPS) * gf_ref[...]


def _row_permutation(tm):
    pos = jnp.arange(tm)
    token = (pos % 8) * (tm // 8) + pos // 8
    return (token[:, None] == jnp.arange(tm)[None, :]).astype(BF16)


def _ffn(h2, x1, mod6, w_up, w_conv, b_conv, w_down, g_final):
    B, T, D = x1.shape
    d_ff = w_down.shape[0]
    tm = min(TILES["ffn"], T)
    tf = TILES["ffn_cols"]
    nj = d_ff // tf
    halo = BF16_ROWS
    row = lambda b, i, j: (b, i, 0)
    nt = T // tm

    def h_tile(b, i, j):
        return _tile_ahead(b, i, jnp.where(j > 0, 1, 0), B, nt)

    def h_map(b, i, j):
        return (*h_tile(b, i, j), 0)

    def hp_map(b, i, j):
        bb, ii = h_tile(b, i, j)
        return (bb, jnp.maximum(ii * (tm // halo) - 1, 0), 0)

    def hn_map(b, i, j):
        bb, ii = h_tile(b, i, j)
        return (bb, jnp.minimum((ii + 1) * (tm // halo), T // halo - 1), 0)

    def x1_map(b, i, j):
        return (*_tile_ahead(b, i, jnp.where(j == 0, -1, 0), B, nt), 0)

    return pl.pallas_call(
        functools.partial(_ffn_kernel, tm=tm),
        grid=(B, T // tm, nj),
        in_specs=[
            pl.BlockSpec((None, tm, D), h_map),
            pl.BlockSpec((None, halo, D), hp_map),
            pl.BlockSpec((None, halo, D), hn_map),
            pl.BlockSpec((D, tf), lambda b, i, j: (0, j)),
            pl.BlockSpec((D, tf), lambda b, i, j: (0, nj + j)),
            pl.BlockSpec((3, tf), lambda b, i, j: (0, j)),
            pl.BlockSpec((3, tf), lambda b, i, j: (0, nj + j)),
            pl.BlockSpec((1, tf), lambda b, i, j: (0, j)),
            pl.BlockSpec((1, tf), lambda b, i, j: (0, nj + j)),
            pl.BlockSpec((tf, D), lambda b, i, j: (j, 0)),
            pl.BlockSpec((None, tm, D), x1_map),
            pl.BlockSpec((None, 1, D), lambda b, i, j: (b * 6 + 5, 0, 0)),
            pl.BlockSpec((1, D), lambda b, i, j: (0, 0)),
            pl.BlockSpec((tm, tm), lambda b, i, j: (0, 0)),
        ],
        out_specs=pl.BlockSpec((None, tm, D), row),
        out_shape=jax.ShapeDtypeStruct((B, T, D), F32),
        scratch_shapes=[pltpu.VMEM((tm + halo, D), BF16),
                        pltpu.VMEM((2, tm + halo, tf), F32),
                        pltpu.VMEM((D // LANES, tm, LANES), F32)],
        compiler_params=_params(("arbitrary", "arbitrary", "arbitrary")),
        name="ffn",
    )(h2, h2, h2, w_up, w_up, w_conv, w_conv, b_conv, b_conv, w_down, x1, mod6, g_final, _row_permutation(tm))


def _rope_tables(T):
    rows = T // GRID_W
    axis_dim = HEAD_DIM // 2
    row = jnp.repeat(jnp.arange(rows, dtype=F32), GRID_W)
    col = jnp.tile(jnp.arange(GRID_W, dtype=F32), rows)
    inv = ROPE_THETA ** (-jnp.arange(0, axis_dim, 2, dtype=F32) / axis_dim)
    ar = row[:, None] * inv
    ac = col[:, None] * inv
    ang = jnp.concatenate([ar, ar, ac, ac], axis=-1)
    return jnp.cos(ang), jnp.sin(ang)


def _rope_matrices():
    j = jnp.arange(IN_TN)
    src, dst = j[:, None], j[None, :]
    half = HEAD_DIM // 4
    lower = (dst % (2 * half)) < half
    rot = jnp.where(lower & (src == dst + half), -1.0, 0.0) + jnp.where(~lower & (src == dst - half), 1.0, 0.0)
    ones = jnp.where(src // HEAD_DIM == dst // HEAD_DIM, 1.0, 0.0)
    return ones.astype(BF16), rot.astype(BF16)


def _prep_weights(w_mod, b_mod, g_mix_norm, w_in, g_q, g_k, w_a_up_f, b_a_f, w_a_up_b, b_a_b, g_gla, w_br_att,
                  w_br_gla, w_out, g_ffn_norm, w_up, w_conv, b_conv, w_down, g_final):
    D = w_in.shape[1]
    low0 = ATT_COLS + QK_COLS + VR_COLS
    low1 = low0 + 2 * GLA_RANK
    wi = w_in[0]
    w_main = jnp.concatenate([wi[:, :low0], wi[:, low1:]], axis=1).astype(BF16)
    w_low = jnp.pad(wi[:, low0:low1], ((0, 0), (0, LANES - 2 * GLA_RANK))).astype(BF16)
    up_f = jnp.pad(w_a_up_f[0], ((0, LANES - GLA_RANK), (0, 0))).astype(BF16)
    up_b = jnp.pad(w_a_up_b[0], ((GLA_RANK, LANES - 2 * GLA_RANK), (0, 0))).astype(BF16)
    return dict(
        w_mod=w_mod[0].astype(BF16), b_mod=b_mod[0], g_mix=g_mix_norm[0].reshape(1, D),
        w_main=w_main, w_low=w_low, g_q=g_q[0].reshape(1, HEAD_DIM) * (HEAD_DIM ** -0.5 * LOG2_E),
        g_k=g_k[0].reshape(1, HEAD_DIM),
        up_f=up_f, up_b=up_b, b_f=b_a_f[0].reshape(1, GLA_QK), b_b=b_a_b[0].reshape(1, GLA_QK),
        g_gla=g_gla[0].reshape(1, GLA_V), w_br_att=w_br_att[0].astype(BF16), w_br_gla=w_br_gla[0].astype(BF16),
        w_out=w_out[0].astype(BF16), g_ffn=g_ffn_norm[0].reshape(1, D), w_up=w_up[0].astype(BF16),
        w_conv=w_conv[0], b_conv=b_conv[0].reshape(1, -1), w_down=w_down[0].astype(BF16),
        g_final=g_final.reshape(1, D))


def _trunk(x, mod, W):
    B, T, D = x.shape
    mod6 = mod.reshape(B * 6, 1, D)
    cos, sin = _rope_tables(T)
    ones_bd, rot_bd = _rope_matrices()
    qkv, qk, vr, gates, low = _inproj(x, mod6, W["g_mix"], W["w_main"], W["w_low"], W["g_q"], W["g_k"],
                                      cos, sin, ones_bd, rot_bd)
    att = _attention(qkv)
    o_f, o_b = _gla(qk, vr, low, W["up_f"], W["b_f"], W["up_b"], W["b_b"])
    x1, h2 = _merge(att, o_f, o_b, vr, gates, x, mod6, W["g_gla"], W["g_ffn"], W["w_br_att"], W["w_br_gla"],
                    W["w_out"])
    return _ffn(h2, x1, mod6, W["w_up"], W["w_conv"], W["b_conv"], W["w_down"], W["g_final"])


def kernel(x_prompt, x_sample, c_prompt, c_sample, w_mod, b_mod, g_mix_norm, w_in, g_q, g_k, w_a_up_f, b_a_f,
           w_a_up_b, b_a_b, g_gla, w_br_att, w_br_gla, w_out, g_ffn_norm, w_up, w_conv, b_conv, w_down, g_final):
    assert w_mod.shape[0] == 1, "single-layer trunk"
    W = _prep_weights(w_mod, b_mod, g_mix_norm, w_in, g_q, g_k, w_a_up_f, b_a_f, w_a_up_b, b_a_b, g_gla, w_br_att,
                      w_br_gla, w_out, g_ffn_norm, w_up, w_conv, b_conv, w_down, g_final)
    nb = x_prompt.shape[0]
    mod = _mod(jnp.concatenate([c_prompt, c_sample], axis=0), W["w_mod"], W["b_mod"])
    return _trunk(x_prompt, mod[:nb], W), _trunk(x_sample, mod[nb:], W)
```

```python
import functools

import jax
import jax.numpy as jnp
from jax import lax
from jax.experimental import pallas as pl
from jax.experimental.pallas import tpu as pltpu

F32 = jnp.float32
BF16 = jnp.bfloat16
EPS = 1e-6
LOG2_E = 1.4426950408889634

HEAD_DIM = 128
ATT_HEADS = 8
ATT_KV_HEADS = 2
ATT_GROUP = ATT_HEADS // ATT_KV_HEADS
ATT_W = ATT_HEADS * HEAD_DIM
ATT_KV_W = ATT_KV_HEADS * HEAD_DIM
GRID_W = 64
ROPE_THETA = 10000.0
GLA_HEADS = 4
GLA_DK = 128
GLA_DV = 256
GLA_QK = GLA_HEADS * GLA_DK
GLA_V = GLA_HEADS * GLA_DV
GLA_RANK = 16
GATE_NORM = 16.0
GLA_CHUNK = 64
LANES = 128
BF16_ROWS = 16
VMEM_LIMIT = 56 * 1024 * 1024

IN_TN = 512
ATT_COLS = ATT_W + 2 * ATT_KV_W
QK_COLS = 2 * GLA_QK
VR_COLS = 2 * GLA_V
ATT_TILES = ATT_COLS // IN_TN
QK_TILES = QK_COLS // IN_TN
VR_TILES = VR_COLS // IN_TN
QK_END = ATT_TILES + QK_TILES
VR_END = QK_END + VR_TILES

TILES = dict(inproj=1024, attn=1024, gla_rows=256, merge=256, ffn=512, ffn_cols=512)
ATT_UNIT_ROWS = 128
FFN_UNIT_ROWS = 128

NT = (((1,), (1,)), ((), ()))
TN = (((0,), (0,)), ((), ()))


def _params(sem):
    return pltpu.CompilerParams(dimension_semantics=sem, vmem_limit_bytes=VMEM_LIMIT)


def _silu(v):
    return v * jax.nn.sigmoid(v)


def _tile_ahead(b, i, ahead, n_batch, n_tiles):
    lin = jnp.clip(b * n_tiles + i + ahead, 0, n_batch * n_tiles - 1)
    return lin // n_tiles, lin % n_tiles


def _mod_kernel(c_ref, w_ref, b_ref, o_ref):
    s = _silu(c_ref[...]).astype(BF16)
    o_ref[...] = jnp.dot(s, w_ref[...], preferred_element_type=F32) + b_ref[...]


def _mod(c, w_mod, b_mod):
    n, d = c.shape
    cols = w_mod.shape[1]
    tn = 2048
    return pl.pallas_call(
        _mod_kernel,
        grid=(cols // tn,),
        in_specs=[pl.BlockSpec((n, d), lambda j: (0, 0)),
                  pl.BlockSpec((d, tn), lambda j: (0, j)),
                  pl.BlockSpec((1, tn), lambda j: (0, j))],
        out_specs=pl.BlockSpec((n, tn), lambda j: (0, j)),
        out_shape=jax.ShapeDtypeStruct((n, cols), F32),
        compiler_params=_params(("arbitrary",)),
        name="mod",
    )(c, w_mod, b_mod.reshape(1, cols))


def _norm_rope(a, g, cos, sin, ones_bd, rot_bd):
    heads = a.shape[1] // HEAD_DIM
    ssq = jnp.dot((a * a).astype(BF16), ones_bd, preferred_element_type=F32)
    y = a * lax.rsqrt(ssq * (1.0 / HEAD_DIM) + EPS) * jnp.concatenate([g] * heads, axis=1)
    rot = jnp.dot(y.astype(BF16), rot_bd, preferred_element_type=F32)
    return y * jnp.concatenate([cos] * heads, axis=1) + rot * jnp.concatenate([sin] * heads, axis=1)


def _inproj_kernel(x_ref, sc_ref, sh_ref, g_ref, w_ref, wlow_ref, gq_ref, gk_ref, cos_ref, sin_ref, ones_ref, rot_ref,
                   att_ref, qk_ref, vr_ref, gate_ref, low_ref, h_scr):
    n = pl.program_id(2)

    @pl.when(n == 0)
    def _():
        x = x_ref[...]
        ms = jnp.mean(x * x, axis=-1, keepdims=True)
        h = x * lax.rsqrt(ms + EPS) * g_ref[...]
        hb = (h * (1.0 + sc_ref[...]) + sh_ref[...]).astype(BF16)
        h_scr[...] = hb
        low_ref[...] = jnp.dot(hb, wlow_ref[...], preferred_element_type=F32)

    def proj():
        return jnp.dot(h_scr[...], w_ref[...], preferred_element_type=F32)

    @pl.when(n < ATT_W // IN_TN)
    def _():
        att_ref[...] = _norm_rope(proj(), gq_ref[...], cos_ref[...], sin_ref[...], ones_ref[...],
                                  rot_ref[...]).astype(BF16)

    @pl.when(n == ATT_W // IN_TN)
    def _():
        acc = proj()
        att_ref[:, :ATT_KV_W] = _norm_rope(acc[:, :ATT_KV_W], gk_ref[...], cos_ref[...], sin_ref[...],
                                           ones_ref[:ATT_KV_W, :ATT_KV_W], rot_ref[:ATT_KV_W, :ATT_KV_W]).astype(BF16)
        att_ref[:, ATT_KV_W:] = acc[:, ATT_KV_W:].astype(BF16)

    @pl.when(n == ATT_TILES)
    def _():
        qk_ref[...] = proj() * (GLA_DK ** -0.5)

    @pl.when((n > ATT_TILES) & (n < QK_END))
    def _():
        qk_ref[...] = proj()

    @pl.when((n >= QK_END) & (n < VR_END))
    def _():
        vr_ref[...] = proj().astype(BF16)

    @pl.when(n >= VR_END)
    def _():
        gate_ref[...] = proj().astype(BF16)


def _inproj(x, mod6, g_mix, w_main, w_low, g_q, g_k, cos, sin, ones_bd, rot_bd):
    B, T, D = x.shape
    tm = min(TILES["inproj"], T)
    n_tiles = w_main.shape[1] // IN_TN
    gate_tiles = n_tiles - VR_END
    row = lambda b, i, n: (b, i, 0)
    vec = lambda b, i, n: (0, 0)
    tab = lambda b, i, n: (i, 0)

    def x_map(b, i, n):
        return (*_tile_ahead(b, i, jnp.where(n > 0, 1, 0), B, T // tm), 0)

    return pl.pallas_call(
        _inproj_kernel,
        grid=(B, T // tm, n_tiles),
        in_specs=[
            pl.BlockSpec((None, tm, D), x_map),
            pl.BlockSpec((None, 1, D), lambda b, i, n: (b * 6 + 1, 0, 0)),
            pl.BlockSpec((None, 1, D), lambda b, i, n: (b * 6 + 0, 0, 0)),
            pl.BlockSpec((1, D), vec),
            pl.BlockSpec((D, IN_TN), lambda b, i, n: (0, n)),
            pl.BlockSpec((D, LANES), vec),
            pl.BlockSpec((1, HEAD_DIM), vec),
            pl.BlockSpec((1, HEAD_DIM), vec),
            pl.BlockSpec((tm, HEAD_DIM), tab),
            pl.BlockSpec((tm, HEAD_DIM), tab),
            pl.BlockSpec((IN_TN, IN_TN), vec),
            pl.BlockSpec((IN_TN, IN_TN), vec),
        ],
        out_specs=[
            pl.BlockSpec((None, tm, IN_TN), lambda b, i, n: (b, i, jnp.minimum(n, ATT_TILES - 1))),
            pl.BlockSpec((None, tm, IN_TN), lambda b, i, n: (b, i, jnp.clip(n - ATT_TILES, 0, QK_TILES - 1))),
            pl.BlockSpec((None, tm, IN_TN), lambda b, i, n: (b, i, jnp.clip(n - QK_END, 0, VR_TILES - 1))),
            pl.BlockSpec((None, tm, IN_TN), lambda b, i, n: (b, i, jnp.clip(n - VR_END, 0, gate_tiles - 1))),
            pl.BlockSpec((None, tm, LANES), row),
        ],
        out_shape=[
            jax.ShapeDtypeStruct((B, T, ATT_COLS), BF16),
            jax.ShapeDtypeStruct((B, T, QK_COLS), F32),
            jax.ShapeDtypeStruct((B, T, VR_COLS), BF16),
            jax.ShapeDtypeStruct((B, T, gate_tiles * IN_TN), BF16),
            jax.ShapeDtypeStruct((B, T, LANES), F32),
        ],
        scratch_shapes=[pltpu.VMEM((tm, D), BF16)],
        compiler_params=_params(("arbitrary", "arbitrary", "arbitrary")),
        name="inproj",
    )(x, mod6, mod6, g_mix, w_main, w_low, g_q, g_k, cos, sin, ones_bd, rot_bd)


def _attn_kernel(q_ref, k_ref, v_ref, o_ref, v_ext):
    @pl.when(pl.program_id(2) == 0)
    def _():
        v_ext[:, :HEAD_DIM] = v_ref[...]
        v_ext[:, HEAD_DIM:] = jnp.ones((v_ext.shape[0], HEAD_DIM), BF16)

    k = k_ref[...]
    v = v_ext[...]
    for g in range(ATT_GROUP):
        cs = slice(g * HEAD_DIM, (g + 1) * HEAD_DIM)
        for r0 in range(0, q_ref.shape[0], ATT_UNIT_ROWS):
            rows = slice(r0, r0 + ATT_UNIT_ROWS)
            s = lax.dot_general(q_ref[rows, cs], k, NT, preferred_element_type=F32)
            m = jnp.max(s, axis=-1, keepdims=True)
            p = jnp.exp2((s - m).astype(BF16))
            o = jnp.dot(p, v, preferred_element_type=F32)
            o_ref[rows, cs] = (o[:, :HEAD_DIM] / o[:, HEAD_DIM:]).astype(BF16)


def _attention(qkv):
    B, T, _ = qkv.shape
    tq = min(TILES["attn"], T)
    gw = ATT_GROUP * HEAD_DIM
    k0 = ATT_W // HEAD_DIM
    v0 = (ATT_W + ATT_KV_W) // HEAD_DIM
    return pl.pallas_call(
        _attn_kernel,
        grid=(B, ATT_KV_HEADS, T // tq),
        in_specs=[
            pl.BlockSpec((None, tq, gw), lambda b, kv, i: (b, i, kv)),
            pl.BlockSpec((None, T, HEAD_DIM), lambda b, kv, i: (b, 0, k0 + kv)),
            pl.BlockSpec((None, T, HEAD_DIM), lambda b, kv, i: (b, 0, v0 + kv)),
        ],
        out_specs=pl.BlockSpec((None, tq, gw), lambda b, kv, i: (b, i, kv)),
        out_shape=jax.ShapeDtypeStruct((B, T, ATT_W), BF16),
        scratch_shapes=[pltpu.VMEM((T, 2 * HEAD_DIM), BF16)],
        compiler_params=_params(("arbitrary", "arbitrary", "arbitrary")),
        name="attn",
    )(qkv, qkv, qkv)


def _log_sigmoid(z):
    return jnp.minimum(z, 0.0) - jnp.log(1.0 + jnp.exp(-jnp.abs(z)))


def _gla_scan(q_ref, k_ref, v_ref, low_ref, w_ref, b_ref, o_ref, s_scr, d, forward, sub):
    C = GLA_CHUNK
    R = sub * C
    t_i = lax.broadcasted_iota(jnp.int32, (R, R), 0)
    s_i = lax.broadcasted_iota(jnp.int32, (R, R), 1)
    same = (t_i // C) == (s_i // C)
    if forward:
        cum = same & (s_i <= t_i)
        keep_same = cum
        keep_cross = (s_i // C) < (t_i // C)
    else:
        cum = same & (s_i >= t_i)
        keep_same = same & (s_i > t_i)
        keep_cross = (s_i // C) > (t_i // C)
    cum = jnp.where(cum, 1.0, 0.0).astype(BF16)

    z = jnp.dot(low_ref[...].astype(BF16), w_ref[...], preferred_element_type=F32) + b_ref[...]
    yield
    la = _log_sigmoid(z) * (LOG2_E / GATE_NORM)
    hi = la.astype(BF16)
    r1 = la - hi.astype(F32)
    mid = r1.astype(BF16)
    lo = (r1 - mid.astype(F32)).astype(BF16)
    yield
    bc = (jnp.dot(cum, hi, preferred_element_type=F32) + jnp.dot(cum, mid, preferred_element_type=F32)
          + jnp.dot(cum, lo, preferred_element_type=F32))
    yield

    order = list(range(sub)) if forward else list(reversed(range(sub)))
    chunk = [bc[c * C:(c + 1) * C, :] for c in range(sub)]
    tot = [ch[C - 1:C, :] if forward else ch[0:1, :] for ch in chunk]
    before, run = {}, jnp.zeros_like(tot[0])
    for c in order:
        before[c] = run
        run = run + tot[c]
    whole = run
    anchor = before[order[sub // 2]]

    def per_chunk(fn):
        return jnp.concatenate([fn(c) for c in range(sub)], axis=0)

    e_q_same = jnp.exp2(bc)
    e_k_same = jnp.exp2(-bc)
    e_q_cross = jnp.exp2(per_chunk(lambda c: chunk[c] + (before[c] - anchor)))
    e_k_cross = jnp.exp2(per_chunk(lambda c: (tot[c] - chunk[c]) + (anchor - before[c] - tot[c])))
    e_q_state = jnp.exp2(per_chunk(lambda c: chunk[c] + before[c]))
    e_k_state = jnp.exp2(per_chunk(lambda c: (tot[c] - chunk[c]) + (whole - before[c] - tot[c])))
    e_whole = jnp.exp2(whole)
    yield

    for h in range(GLA_HEADS):
        ks = slice(h * GLA_DK, (h + 1) * GLA_DK)
        vs = slice(h * GLA_DV, (h + 1) * GLA_DV)
        q = q_ref[:, ks]
        k = k_ref[:, ks]
        v = v_ref[:, vs]
        a_same = lax.dot_general((q * e_q_same[:, ks]).astype(BF16), (k * e_k_same[:, ks]).astype(BF16), NT,
                                 preferred_element_type=F32)
        a_cross = lax.dot_general((q * e_q_cross[:, ks]).astype(BF16), (k * e_k_cross[:, ks]).astype(BF16), NT,
                                  preferred_element_type=F32)
        yield
        a = jnp.where(keep_same, a_same, jnp.where(keep_cross, a_cross, 0.0)).astype(BF16)
        st = s_scr[d, h]
        yield
        o_ref[:, vs] = (jnp.dot(a, v, preferred_element_type=F32)
                        + lax.dot_general((q * e_q_state[:, ks]).astype(BF16), st.astype(BF16), NT,
                                          preferred_element_type=F32))
        s_scr[d, h] = st * e_whole[:, ks] + lax.dot_general(v, (k * e_k_state[:, ks]).astype(BF16), TN,
                                                            preferred_element_type=F32)
        yield


def _alternate(stage_generators):
    live = list(stage_generators)
    while live:
        for g in list(live):
            try:
                next(g)
            except StopIteration:
                live.remove(g)


def _gla_kernel(qf, kf, vf, lf, qb, kb, vb, lb, wf, wb, bf_, bb_, of, ob, s_scr, *, sub):
    @pl.when(pl.program_id(1) == 0)
    def _():
        s_scr[...] = jnp.zeros_like(s_scr)

    scans = [_gla_scan(qf, kf, vf, lf, wf, bf_, of, s_scr, 0, True, sub),
             _gla_scan(qb, kb, vb, lb, wb, bb_, ob, s_scr, 1, False, sub)]
    _alternate(scans)


def _gla(qk, vr, low, w_up_f, b_f, w_up_b, b_b):
    B, T, _ = qk.shape
    rb = min(TILES["gla_rows"], T)
    nb = T // rb
    fwd = lambda b, c: (b, c, 0)
    bwd = lambda b, c: (b, nb - 1 - c, 0)
    vec = lambda b, c: (0, 0)

    def streams(row_map):
        r = lambda col: (lambda b, c: (*row_map(b, c)[:2], col))
        return [pl.BlockSpec((None, rb, GLA_QK), r(0)),
                pl.BlockSpec((None, rb, GLA_QK), r(1)),
                pl.BlockSpec((None, rb, GLA_V), r(0)),
                pl.BlockSpec((None, rb, LANES), r(0))]

    return pl.pallas_call(
        functools.partial(_gla_kernel, sub=rb // GLA_CHUNK),
        grid=(B, nb),
        in_specs=streams(fwd) + streams(bwd) + [
            pl.BlockSpec((LANES, GLA_QK), vec), pl.BlockSpec((LANES, GLA_QK), vec),
            pl.BlockSpec((1, GLA_QK), vec), pl.BlockSpec((1, GLA_QK), vec)],
        out_specs=[pl.BlockSpec((None, rb, GLA_V), fwd), pl.BlockSpec((None, rb, GLA_V), bwd)],
        out_shape=[jax.ShapeDtypeStruct((B, T, GLA_V), F32)] * 2,
        scratch_shapes=[pltpu.VMEM((2, GLA_HEADS, GLA_DV, GLA_DK), F32)],
        compiler_params=_params(("arbitrary", "arbitrary")),
        name="gla",
    )(qk, qk, vr, low, qk, qk, vr, low, w_up_f, w_up_b, b_f, b_b)


def _merge_kernel(att_ref, of_ref, ob_ref, r_ref, ga_ref, gg_ref, x_ref, gt_ref, sc_ref, sh_ref, ggla_ref, gffn_ref,
                  wa_ref, wg_ref, wo_ref, x1_ref, h2_ref):
    parts = []
    for h in range(GLA_HEADS):
        vs = slice(h * GLA_DV, (h + 1) * GLA_DV)
        o = of_ref[:, vs] + ob_ref[:, vs]
        ms = jnp.mean(o * o, axis=-1, keepdims=True)
        y = o * lax.rsqrt(ms + EPS) * ggla_ref[:, vs]
        parts.append((y * _silu(r_ref[:, vs].astype(F32))).astype(BF16))
    gl = jnp.concatenate(parts, axis=-1)
    a = jnp.dot(att_ref[...], wa_ref[...], preferred_element_type=F32)
    g = jnp.dot(gl, wg_ref[...], preferred_element_type=F32)
    gate_a = jax.nn.sigmoid(ga_ref[...].astype(F32))
    gate_g = jax.nn.sigmoid(gg_ref[...].astype(F32))
    merged = (gate_a * a + gate_g * g).astype(BF16)
    out = jnp.dot(merged, wo_ref[...], preferred_element_type=F32)
    x1 = x_ref[...] + gt_ref[...] * out
    x1_ref[...] = x1
    ms = jnp.mean(x1 * x1, axis=-1, keepdims=True)
    h2 = x1 * lax.rsqrt(ms + EPS) * gffn_ref[...]
    h2_ref[...] = (h2 * (1.0 + sc_ref[...]) + sh_ref[...]).astype(BF16)


def _merge(att, o_f, o_b, vr, gates, x, mod6, g_gla, g_ffn, w_br_att, w_br_gla, w_out):
    B, T, D = x.shape
    tm = min(TILES["merge"], T)
    row = lambda b, i: (b, i, 0)
    vec = lambda b, i: (0, 0)
    resident = lambda shape: pl.BlockSpec(shape, vec, pipeline_mode=pl.Buffered(1))
    return pl.pallas_call(
        _merge_kernel,
        grid=(B, T // tm),
        in_specs=[
            pl.BlockSpec((None, tm, ATT_W), row),
            pl.BlockSpec((None, tm, GLA_V), row),
            pl.BlockSpec((None, tm, GLA_V), row),
            pl.BlockSpec((None, tm, GLA_V), lambda b, i: (b, i, 1)),
            pl.BlockSpec((None, tm, D), lambda b, i: (b, i, 0)),
            pl.BlockSpec((None, tm, D), lambda b, i: (b, i, 1)),
            pl.BlockSpec((None, tm, D), row),
            pl.BlockSpec((None, 1, D), lambda b, i: (b * 6 + 2, 0, 0)),
            pl.BlockSpec((None, 1, D), lambda b, i: (b * 6 + 4, 0, 0)),
            pl.BlockSpec((None, 1, D), lambda b, i: (b * 6 + 3, 0, 0)),
            pl.BlockSpec((1, GLA_V), vec),
            pl.BlockSpec((1, D), vec),
            resident((ATT_W, D)),
            resident((GLA_V, D)),
            resident((D, D)),
        ],
        out_specs=[pl.BlockSpec((None, tm, D), row), pl.BlockSpec((None, tm, D), row)],
        out_shape=[jax.ShapeDtypeStruct((B, T, D), F32), jax.ShapeDtypeStruct((B, T, D), BF16)],
        compiler_params=_params(("arbitrary", "arbitrary")),
        name="merge",
    )(att, o_f, o_b, vr, gates, gates, x, mod6, mod6, mod6, g_gla, g_ffn, w_br_att, w_br_gla, w_out)


def _ffn_kernel(h_ref, hp_ref, hn_ref, wv_ref, wg_ref, cv_ref, cg_ref, bv_ref, bg_ref, wd_ref, x1_ref, gt_ref,
                gf_ref, perm_ref, y_ref, hext, u_scr, acc, *, tm):
    i = pl.program_id(1)
    j = pl.program_id(2)
    halo = BF16_ROWS
    sub = 8
    G = tm // sub
    lane_tiles = acc.shape[0]

    def lanes(c):
        return slice(c * LANES, (c + 1) * LANES)

    def gather_rows(ref, start, stride):
        return jnp.concatenate([ref[c, pl.ds(start, sub, stride=stride), :] for c in range(lane_tiles)], axis=1)

    @pl.when(j == 0)
    def _():
        has_prev = (i > 0).astype(F32)
        has_next = (i < pl.num_programs(1) - 1).astype(F32)
        before = hp_ref[...].astype(F32)[halo - 1:halo] * has_prev
        after = hn_ref[...].astype(F32)[0:1] * has_next
        edge = jnp.concatenate([before, after, jnp.zeros((halo - 2, before.shape[1]), F32)], axis=0)
        hext[tm:, :] = edge.astype(BF16)
        hext[:tm, :] = jnp.dot(perm_ref[...], h_ref[...], preferred_element_type=F32).astype(BF16)
        acc[...] = jnp.zeros_like(acc)

    row_id = lax.broadcasted_iota(jnp.int32, (sub, wv_ref.shape[1]), 0)
    unit = min(FFN_UNIT_ROWS, tm)
    n_units = tm // unit

    def up(q):
        r0, r1 = q * unit, (q + 1) * unit + (halo if q == n_units - 1 else 0)
        lhs = hext[r0:r1, :]
        u_scr[0, r0:r1, :] = jnp.dot(lhs, wv_ref[...], preferred_element_type=F32)
        u_scr[1, r0:r1, :] = jnp.dot(lhs, wg_ref[...], preferred_element_type=F32)

    def conv(q, k, c_ref, b_ref):
        r0, r1 = q * unit, (q + 1) * unit
        main = u_scr[k, r0:r1, :]
        if q > 0:
            prev = u_scr[k, r0 - sub:r1 - sub, :]
        else:
            before = u_scr[k, tm:tm + 1, :]
            first = jnp.where(row_id == 0, before, pltpu.roll(u_scr[k, tm - sub:tm, :], 1, axis=0))
            prev = jnp.concatenate([first, u_scr[k, 0:r1 - sub, :]], axis=0)
        if q < n_units - 1:
            nxt = u_scr[k, r0 + sub:r1 + sub, :]
        else:
            after = u_scr[k, tm + 1:tm + 2, :]
            last = jnp.where(row_id == sub - 1, after, pltpu.roll(u_scr[k, 0:sub, :], sub - 1, axis=0))
            nxt = jnp.concatenate([u_scr[k, r0 + sub:tm, :], last], axis=0)
        return c_ref[0:1, :] * prev + c_ref[1:2, :] * main + c_ref[2:3, :] * nxt + b_ref[...]

    def gate_down(q):
        r0, r1 = q * unit, (q + 1) * unit
        act = (_silu(conv(q, 1, cg_ref, bg_ref)) * conv(q, 0, cv_ref, bv_ref)).astype(BF16)
        down = jnp.dot(act, wd_ref[...], preferred_element_type=F32)
        for c in range(lane_tiles):
            acc[c, r0:r1, :] += down[:, lanes(c)]

    issued = set()

    def ensure_up(q):
        if q not in issued:
            issued.add(q)
            up(q)

    ensure_up(n_units - 1)
    for q in range(n_units):
        for ahead in range(q, min(q + 3, n_units)):
            ensure_up(ahead)
        gate_down(q)

    @pl.when(j == pl.num_programs(2) - 1)
    def _():
        for g in range(G):
            rows = slice(g * sub, (g + 1) * sub)
            ffn = gather_rows(acc, (g % (G // sub)) * sub * sub + g // (G // sub), sub)
            x2 = x1_ref[rows, :] + gt_ref[...] * ffn
            ms = jnp.mean(x2 * x2, axis=-1, keepdims=True)
            y_ref[rows, :] = x2 * lax.rsqrt(ms + EPS) * gf_ref[...]


def _row_permutation(tm):
    pos = jnp.arange(tm)
    token = (pos % 8) * (tm // 8) + pos // 8
    return (token[:, None] == jnp.arange(tm)[None, :]).astype(BF16)


def _ffn(h2, x1, mod6, w_up, w_conv, b_conv, w_down, g_final):
    B, T, D = x1.shape
    d_ff = w_down.shape[0]
    tm = min(TILES["ffn"], T)
    tf = TILES["ffn_cols"]
    nj = d_ff // tf
    halo = BF16_ROWS
    row = lambda b, i, j: (b, i, 0)
    nt = T // tm

    def h_tile(b, i, j):
        return _tile_ahead(b, i, jnp.where(j > 0, 1, 0), B, nt)

    def h_map(b, i, j):
        return (*h_tile(b, i, j), 0)

    def hp_map(b, i, j):
        bb, ii = h_tile(b, i, j)
        return (bb, jnp.maximum(ii * (tm // halo) - 1, 0), 0)

    def hn_map(b, i, j):
        bb, ii = h_tile(b, i, j)
        return (bb, jnp.minimum((ii + 1) * (tm // halo), T // halo - 1), 0)

    def x1_map(b, i, j):
        return (*_tile_ahead(b, i, jnp.where(j == 0, -1, 0), B, nt), 0)

    return pl.pallas_call(
        functools.partial(_ffn_kernel, tm=tm),
        grid=(B, T // tm, nj),
        in_specs=[
            pl.BlockSpec((None, tm, D), h_map),
            pl.BlockSpec((None, halo, D), hp_map),
            pl.BlockSpec((None, halo, D), hn_map),
            pl.BlockSpec((D, tf), lambda b, i, j: (0, j)),
            pl.BlockSpec((D, tf), lambda b, i, j: (0, nj + j)),
            pl.BlockSpec((3, tf), lambda b, i, j: (0, j)),
            pl.BlockSpec((3, tf), lambda b, i, j: (0, nj + j)),
            pl.BlockSpec((1, tf), lambda b, i, j: (0, j)),
            pl.BlockSpec((1, tf), lambda b, i, j: (0, nj + j)),
            pl.BlockSpec((tf, D), lambda b, i, j: (j, 0)),
            pl.BlockSpec((None, tm, D), x1_map),
            pl.BlockSpec((None, 1, D), lambda b, i, j: (b * 6 + 5, 0, 0)),
            pl.BlockSpec((1, D), lambda b, i, j: (0, 0)),
            pl.BlockSpec((tm, tm), lambda b, i, j: (0, 0)),
        ],
        out_specs=pl.BlockSpec((None, tm, D), row),
        out_shape=jax.ShapeDtypeStruct((B, T, D), F32),
        scratch_shapes=[pltpu.VMEM((tm + halo, D), BF16),
                        pltpu.VMEM((2, tm + halo, tf), F32),
                        pltpu.VMEM((D // LANES, tm, LANES), F32)],
        compiler_params=_params(("arbitrary", "arbitrary", "arbitrary")),
        name="ffn",
    )(h2, h2, h2, w_up, w_up, w_conv, w_conv, b_conv, b_conv, w_down, x1, mod6, g_final, _row_permutation(tm))


def _rope_tables(T):
    rows = T // GRID_W
    axis_dim = HEAD_DIM // 2
    row = jnp.repeat(jnp.arange(rows, dtype=F32), GRID_W)
    col = jnp.tile(jnp.arange(GRID_W, dtype=F32), rows)
    inv = ROPE_THETA ** (-jnp.arange(0, axis_dim, 2, dtype=F32) / axis_dim)
    ar = row[:, None] * inv
    ac = col[:, None] * inv
    ang = jnp.concatenate([ar, ar, ac, ac], axis=-1)
    return jnp.cos(ang), jnp.sin(ang)


def _rope_matrices():
    j = jnp.arange(IN_TN)
    src, dst = j[:, None], j[None, :]
    half = HEAD_DIM // 4
    lower = (dst % (2 * half)) < half
    rot = jnp.where(lower & (src == dst + half), -1.0, 0.0) + jnp.where(~lower & (src == dst - half), 1.0, 0.0)
    ones = jnp.where(src // HEAD_DIM == dst // HEAD_DIM, 1.0, 0.0)
    return ones.astype(BF16), rot.astype(BF16)


def _prep_weights(w_mod, b_mod, g_mix_norm, w_in, g_q, g_k, w_a_up_f, b_a_f, w_a_up_b, b_a_b, g_gla, w_br_att,
                  w_br_gla, w_out, g_ffn_norm, w_up, w_conv, b_conv, w_down, g_final):
    D = w_in.shape[1]
    low0 = ATT_COLS + QK_COLS + VR_COLS
    low1 = low0 + 2 * GLA_RANK
    wi = w_in[0]
    w_main = jnp.concatenate([wi[:, :low0], wi[:, low1:]], axis=1).astype(BF16)
    w_low = jnp.pad(wi[:, low0:low1], ((0, 0), (0, LANES - 2 * GLA_RANK))).astype(BF16)
    up_f = jnp.pad(w_a_up_f[0], ((0, LANES - GLA_RANK), (0, 0))).astype(BF16)
    up_b = jnp.pad(w_a_up_b[0], ((GLA_RANK, LANES - 2 * GLA_RANK), (0, 0))).astype(BF16)
    return dict(
        w_mod=w_mod[0].astype(BF16), b_mod=b_mod[0], g_mix=g_mix_norm[0].reshape(1, D),
        w_main=w_main, w_low=w_low, g_q=g_q[0].reshape(1, HEAD_DIM) * (HEAD_DIM ** -0.5 * LOG2_E),
        g_k=g_k[0].reshape(1, HEAD_DIM),
        up_f=up_f, up_b=up_b, b_f=b_a_f[0].reshape(1, GLA_QK), b_b=b_a_b[0].reshape(1, GLA_QK),
        g_gla=g_gla[0].reshape(1, GLA_V), w_br_att=w_br_att[0].astype(BF16), w_br_gla=w_br_gla[0].astype(BF16),
        w_out=w_out[0].astype(BF16), g_ffn=g_ffn_norm[0].reshape(1, D), w_up=w_up[0].astype(BF16),
        w_conv=w_conv[0], b_conv=b_conv[0].reshape(1, -1), w_down=w_down[0].astype(BF16),
        g_final=g_final.reshape(1, D))


def _trunk(x, mod, W):
    B, T, D = x.shape
    mod6 = mod.reshape(B * 6, 1, D)
    cos, sin = _rope_tables(T)
    ones_bd, rot_bd = _rope_matrices()
    qkv, qk, vr, gates, low = _inproj(x, mod6, W["g_mix"], W["w_main"], W["w_low"], W["g_q"], W["g_k"],
                                      cos, sin, ones_bd, rot_bd)
    att = _attention(qkv)
    o_f, o_b = _gla(qk, vr, low, W["up_f"], W["b_f"], W["up_b"], W["b_b"])
    x1, h2 = _merge(att, o_f, o_b, vr, gates, x, mod6, W["g_gla"], W["g_ffn"], W["w_br_att"], W["w_br_gla"],
                    W["w_out"])
    return _ffn(h2, x1, mod6, W["w_up"], W["w_conv"], W["b_conv"], W["w_down"], W["g_final"])


def kernel(x_prompt, x_sample, c_prompt, c_sample, w_mod, b_mod, g_mix_norm, w_in, g_q, g_k, w_a_up_f, b_a_f,
           w_a_up_b, b_a_b, g_gla, w_br_att, w_br_gla, w_out, g_ffn_norm, w_up, w_conv, b_conv, w_down, g_final):
    assert w_mod.shape[0] == 1, "single-layer trunk"
    W = _prep_weights(w_mod, b_mod, g_mix_norm, w_in, g_q, g_k, w_a_up_f, b_a_f, w_a_up_b, b_a_b, g_gla, w_br_att,
                      w_br_gla, w_out, g_ffn_norm, w_up, w_conv, b_conv, w_down, g_final)
    nb = x_prompt.shape[0]
    mod = _mod(jnp.concatenate([c_prompt, c_sample], axis=0), W["w_mod"], W["b_mod"])
    return _trunk(x_prompt, mod[:nb], W), _trunk(x_sample, mod[nb:], W)
```

```python
import functools

import jax
import jax.numpy as jnp
from jax import lax
from jax.experimental import pallas as pl
from jax.experimental.pallas import tpu as pltpu

F32 = jnp.float32
BF16 = jnp.bfloat16
EPS = 1e-6
LOG2_E = 1.4426950408889634

HEAD_DIM = 128
ATT_HEADS = 8
ATT_KV_HEADS = 2
ATT_GROUP = ATT_HEADS // ATT_KV_HEADS
ATT_W = ATT_HEADS * HEAD_DIM
ATT_KV_W = ATT_KV_HEADS * HEAD_DIM
GRID_W = 64
ROPE_THETA = 10000.0
GLA_HEADS = 4
GLA_DK = 128
GLA_DV = 256
GLA_QK = GLA_HEADS * GLA_DK
GLA_V = GLA_HEADS * GLA_DV
GLA_RANK = 16
GATE_NORM = 16.0
GLA_CHUNK = 64
LANES = 128
BF16_ROWS = 16
VMEM_LIMIT = 56 * 1024 * 1024

IN_TN = 512
ATT_COLS = ATT_W + 2 * ATT_KV_W
QK_COLS = 2 * GLA_QK
VR_COLS = 2 * GLA_V
ATT_TILES = ATT_COLS // IN_TN
QK_TILES = QK_COLS // IN_TN
VR_TILES = VR_COLS // IN_TN
QK_END = ATT_TILES + QK_TILES
VR_END = QK_END + VR_TILES

TILES = dict(inproj=1024, attn=1024, gla_rows=256, merge=256, ffn=512, ffn_cols=512)
ATT_UNIT_ROWS = 128
FFN_UNIT_ROWS = 128

NT = (((1,), (1,)), ((), ()))
TN = (((0,), (0,)), ((), ()))


def _params(sem):
    return pltpu.CompilerParams(dimension_semantics=sem, vmem_limit_bytes=VMEM_LIMIT)


def _silu(v):
    return v * jax.nn.sigmoid(v)


def _tile_ahead(b, i, ahead, n_batch, n_tiles):
    lin = jnp.clip(b * n_tiles + i + ahead, 0, n_batch * n_tiles - 1)
    return lin // n_tiles, lin % n_tiles


def _mod_kernel(c_ref, w_ref, b_ref, o_ref):
    s = _silu(c_ref[...]).astype(BF16)
    o_ref[...] = jnp.dot(s, w_ref[...], preferred_element_type=F32) + b_ref[...]


def _mod(c, w_mod, b_mod):
    n, d = c.shape
    cols = w_mod.shape[1]
    tn = 2048
    return pl.pallas_call(
        _mod_kernel,
        grid=(cols // tn,),
        in_specs=[pl.BlockSpec((n, d), lambda j: (0, 0)),
                  pl.BlockSpec((d, tn), lambda j: (0, j)),
                  pl.BlockSpec((1, tn), lambda j: (0, j))],
        out_specs=pl.BlockSpec((n, tn), lambda j: (0, j)),
        out_shape=jax.ShapeDtypeStruct((n, cols), F32),
        compiler_params=_params(("arbitrary",)),
        name="mod",
    )(c, w_mod, b_mod.reshape(1, cols))


def _norm_rope(a, g, cos, sin, ones_bd, rot_bd):
    heads = a.shape[1] // HEAD_DIM
    ssq = jnp.dot((a * a).astype(BF16), ones_bd, preferred_element_type=F32)
    y = a * lax.rsqrt(ssq * (1.0 / HEAD_DIM) + EPS) * jnp.concatenate([g] * heads, axis=1)
    rot = jnp.dot(y.astype(BF16), rot_bd, preferred_element_type=F32)
    return y * jnp.concatenate([cos] * heads, axis=1) + rot * jnp.concatenate([sin] * heads, axis=1)


def _inproj_kernel(x_ref, sc_ref, sh_ref, g_ref, w_ref, wlow_ref, gq_ref, gk_ref, cos_ref, sin_ref, ones_ref, rot_ref,
                   att_ref, qk_ref, vr_ref, gate_ref, low_ref, h_scr):
    n = pl.program_id(2)

    @pl.when(n == 0)
    def _():
        x = x_ref[...]
        ms = jnp.mean(x * x, axis=-1, keepdims=True)
        h = x * lax.rsqrt(ms + EPS) * g_ref[...]
        hb = (h * (1.0 + sc_ref[...]) + sh_ref[...]).astype(BF16)
        h_scr[...] = hb
        low_ref[...] = jnp.dot(hb, wlow_ref[...], preferred_element_type=F32)

    def proj():
        return jnp.dot(h_scr[...], w_ref[...], preferred_element_type=F32)

    @pl.when(n < ATT_W // IN_TN)
    def _():
        att_ref[...] = _norm_rope(proj(), gq_ref[...], cos_ref[...], sin_ref[...], ones_ref[...],
                                  rot_ref[...]).astype(BF16)

    @pl.when(n == ATT_W // IN_TN)
    def _():
        acc = proj()
        att_ref[:, :ATT_KV_W] = _norm_rope(acc[:, :ATT_KV_W], gk_ref[...], cos_ref[...], sin_ref[...],
                                           ones_ref[:ATT_KV_W, :ATT_KV_W], rot_ref[:ATT_KV_W, :ATT_KV_W]).astype(BF16)
        att_ref[:, ATT_KV_W:] = acc[:, ATT_KV_W:].astype(BF16)

    @pl.when(n == ATT_TILES)
    def _():
        qk_ref[...] = proj() * (GLA_DK ** -0.5)

    @pl.when((n > ATT_TILES) & (n < QK_END))
    def _():
        qk_ref[...] = proj()

    @pl.when((n >= QK_END) & (n < VR_END))
    def _():
        vr_ref[...] = proj().astype(BF16)

    @pl.when(n >= VR_END)
    def _():
        gate_ref[...] = proj().astype(BF16)


def _inproj(x, mod6, g_mix, w_main, w_low, g_q, g_k, cos, sin, ones_bd, rot_bd):
    B, T, D = x.shape
    tm = min(TILES["inproj"], T)
    n_tiles = w_main.shape[1] // IN_TN
    gate_tiles = n_tiles - VR_END
    row = lambda b, i, n: (b, i, 0)
    vec = lambda b, i, n: (0, 0)
    tab = lambda b, i, n: (i, 0)

    def x_map(b, i, n):
        return (*_tile_ahead(b, i, jnp.where(n > 0, 1, 0), B, T // tm), 0)

    return pl.pallas_call(
        _inproj_kernel,
        grid=(B, T // tm, n_tiles),
        in_specs=[
            pl.BlockSpec((None, tm, D), x_map),
            pl.BlockSpec((None, 1, D), lambda b, i, n: (b * 6 + 1, 0, 0)),
            pl.BlockSpec((None, 1, D), lambda b, i, n: (b * 6 + 0, 0, 0)),
            pl.BlockSpec((1, D), vec),
            pl.BlockSpec((D, IN_TN), lambda b, i, n: (0, n)),
            pl.BlockSpec((D, LANES), vec),
            pl.BlockSpec((1, HEAD_DIM), vec),
            pl.BlockSpec((1, HEAD_DIM), vec),
            pl.BlockSpec((tm, HEAD_DIM), tab),
            pl.BlockSpec((tm, HEAD_DIM), tab),
            pl.BlockSpec((IN_TN, IN_TN), vec),
            pl.BlockSpec((IN_TN, IN_TN), vec),
        ],
        out_specs=[
            pl.BlockSpec((None, tm, IN_TN), lambda b, i, n: (b, i, jnp.minimum(n, ATT_TILES - 1))),
            pl.BlockSpec((None, tm, IN_TN), lambda b, i, n: (b, i, jnp.clip(n - ATT_TILES, 0, QK_TILES - 1))),
            pl.BlockSpec((None, tm, IN_TN), lambda b, i, n: (b, i, jnp.clip(n - QK_END, 0, VR_TILES - 1))),
            pl.BlockSpec((None, tm, IN_TN), lambda b, i, n: (b, i, jnp.clip(n - VR_END, 0, gate_tiles - 1))),
            pl.BlockSpec((None, tm, LANES), row),
        ],
        out_shape=[
            jax.ShapeDtypeStruct((B, T, ATT_COLS), BF16),
            jax.ShapeDtypeStruct((B, T, QK_COLS), F32),
            jax.ShapeDtypeStruct((B, T, VR_COLS), BF16),
            jax.ShapeDtypeStruct((B, T, gate_tiles * IN_TN), BF16),
            jax.ShapeDtypeStruct((B, T, LANES), F32),
        ],
        scratch_shapes=[pltpu.VMEM((tm, D), BF16)],
        compiler_params=_params(("arbitrary", "arbitrary", "arbitrary")),
        name="inproj",
    )(x, mod6, mod6, g_mix, w_main, w_low, g_q, g_k, cos, sin, ones_bd, rot_bd)


def _attn_kernel(q_ref, k_ref, v_ref, o_ref, v_ext):
    @pl.when(pl.program_id(2) == 0)
    def _():
        v_ext[:, :HEAD_DIM] = v_ref[...]
        v_ext[:, HEAD_DIM:] = jnp.ones((v_ext.shape[0], HEAD_DIM), BF16)

    k = k_ref[...]
    v = v_ext[...]
    for g in range(ATT_GROUP):
        cs = slice(g * HEAD_DIM, (g + 1) * HEAD_DIM)
        for r0 in range(0, q_ref.shape[0], ATT_UNIT_ROWS):
            rows = slice(r0, r0 + ATT_UNIT_ROWS)
            s = lax.dot_general(q_ref[rows, cs], k, NT, preferred_element_type=F32)
            m = jnp.max(s, axis=-1, keepdims=True)
            p = jnp.exp2((s - m).astype(BF16))
            o = jnp.dot(p, v, preferred_element_type=F32)
            o_ref[rows, cs] = (o[:, :HEAD_DIM] / o[:, HEAD_DIM:]).astype(BF16)


def _attention(qkv):
    B, T, _ = qkv.shape
    tq = min(TILES["attn"], T)
    gw = ATT_GROUP * HEAD_DIM
    k0 = ATT_W // HEAD_DIM
    v0 = (ATT_W + ATT_KV_W) // HEAD_DIM
    return pl.pallas_call(
        _attn_kernel,
        grid=(B, ATT_KV_HEADS, T // tq),
        in_specs=[
            pl.BlockSpec((None, tq, gw), lambda b, kv, i: (b, i, kv)),
            pl.BlockSpec((None, T, HEAD_DIM), lambda b, kv, i: (b, 0, k0 + kv)),
            pl.BlockSpec((None, T, HEAD_DIM), lambda b, kv, i: (b, 0, v0 + kv)),
        ],
        out_specs=pl.BlockSpec((None, tq, gw), lambda b, kv, i: (b, i, kv)),
        out_shape=jax.ShapeDtypeStruct((B, T, ATT_W), BF16),
        scratch_shapes=[pltpu.VMEM((T, 2 * HEAD_DIM), BF16)],
        compiler_params=_params(("arbitrary", "arbitrary", "arbitrary")),
        name="attn",
    )(qkv, qkv, qkv)


def _log_sigmoid(z):
    return jnp.minimum(z, 0.0) - jnp.log(1.0 + jnp.exp(-jnp.abs(z)))


def _gla_scan(q_ref, k_ref, v_ref, low_ref, w_ref, b_ref, o_ref, s_scr, d, forward, sub):
    C = GLA_CHUNK
    R = sub * C
    t_i = lax.broadcasted_iota(jnp.int32, (R, R), 0)
    s_i = lax.broadcasted_iota(jnp.int32, (R, R), 1)
    same = (t_i // C) == (s_i // C)
    if forward:
        cum = same & (s_i <= t_i)
        keep_same = cum
        keep_cross = (s_i // C) < (t_i // C)
    else:
        cum = same & (s_i >= t_i)
        keep_same = same & (s_i > t_i)
        keep_cross = (s_i // C) > (t_i // C)
    cum = jnp.where(cum, 1.0, 0.0).astype(BF16)

    z = jnp.dot(low_ref[...].astype(BF16), w_ref[...], preferred_element_type=F32) + b_ref[...]
    yield
    la = _log_sigmoid(z) * (LOG2_E / GATE_NORM)
    hi = la.astype(BF16)
    r1 = la - hi.astype(F32)
    mid = r1.astype(BF16)
    lo = (r1 - mid.astype(F32)).astype(BF16)
    yield
    bc = (jnp.dot(cum, hi, preferred_element_type=F32) + jnp.dot(cum, mid, preferred_element_type=F32)
          + jnp.dot(cum, lo, preferred_element_type=F32))
    yield

    order = list(range(sub)) if forward else list(reversed(range(sub)))
    chunk = [bc[c * C:(c + 1) * C, :] for c in range(sub)]
    tot = [ch[C - 1:C, :] if forward else ch[0:1, :] for ch in chunk]
    before, run = {}, jnp.zeros_like(tot[0])
    for c in order:
        before[c] = run
        run = run + tot[c]
    whole = run
    anchor = before[order[sub // 2]]

    def per_chunk(fn):
        return jnp.concatenate([fn(c) for c in range(sub)], axis=0)

    e_q_same = jnp.exp2(bc)
    e_k_same = jnp.exp2(-bc)
    e_q_cross = jnp.exp2(per_chunk(lambda c: chunk[c] + (before[c] - anchor)))
    e_k_cross = jnp.exp2(per_chunk(lambda c: (tot[c] - chunk[c]) + (anchor - before[c] - tot[c])))
    e_q_state = jnp.exp2(per_chunk(lambda c: chunk[c] + before[c]))
    e_k_state = jnp.exp2(per_chunk(lambda c: (tot[c] - chunk[c]) + (whole - before[c] - tot[c])))
    e_whole = jnp.exp2(whole)
    yield

    for h in range(GLA_HEADS):
        ks = slice(h * GLA_DK, (h + 1) * GLA_DK)
        vs = slice(h * GLA_DV, (h + 1) * GLA_DV)
        q = q_ref[:, ks]
        k = k_ref[:, ks]
        v = v_ref[:, vs]
        a_same = lax.dot_general((q * e_q_same[:, ks]).astype(BF16), (k * e_k_same[:, ks]).astype(BF16), NT,
                                 preferred_element_type=F32)
        a_cross = lax.dot_general((q * e_q_cross[:, ks]).astype(BF16), (k * e_k_cross[:, ks]).astype(BF16), NT,
                                  preferred_element_type=F32)
        yield
        a = jnp.where(keep_same, a_same, jnp.where(keep_cross, a_cross, 0.0)).astype(BF16)
        st = s_scr[d, h]
        yield
        o_ref[:, vs] = (jnp.dot(a, v, preferred_element_type=F32)
                        + lax.dot_general((q * e_q_state[:, ks]).astype(BF16), st.astype(BF16), NT,
                                          preferred_element_type=F32))
        s_scr[d, h] = st * e_whole[:, ks] + lax.dot_general(v, (k * e_k_state[:, ks]).astype(BF16), TN,
                                                            preferred_element_type=F32)
        yield


def _alternate(stage_generators):
    live = list(stage_generators)
    while live:
        for g in list(live):
            try:
                next(g)
            except StopIteration:
                live.remove(g)


def _gla_kernel(qf, kf, vf, lf, qb, kb, vb, lb, wf, wb, bf_, bb_, of, ob, s_scr, *, sub):
    @pl.when(pl.program_id(1) == 0)
    def _():
        s_scr[...] = jnp.zeros_like(s_scr)

    scans = [_gla_scan(qf, kf, vf, lf, wf, bf_, of, s_scr, 0, True, sub),
             _gla_scan(qb, kb, vb, lb, wb, bb_, ob, s_scr, 1, False, sub)]
    _alternate(scans)


def _gla(qk, vr, low, w_up_f, b_f, w_up_b, b_b):
    B, T, _ = qk.shape
    rb = min(TILES["gla_rows"], T)
    nb = T // rb
    fwd = lambda b, c: (b, c, 0)
    bwd = lambda b, c: (b, nb - 1 - c, 0)
    vec = lambda b, c: (0, 0)

    def streams(row_map):
        r = lambda col: (lambda b, c: (*row_map(b, c)[:2], col))
        return [pl.BlockSpec((None, rb, GLA_QK), r(0)),
                pl.BlockSpec((None, rb, GLA_QK), r(1)),
                pl.BlockSpec((None, rb, GLA_V), r(0)),
                pl.BlockSpec((None, rb, LANES), r(0))]

    return pl.pallas_call(
        functools.partial(_gla_kernel, sub=rb // GLA_CHUNK),
        grid=(B, nb),
        in_specs=streams(fwd) + streams(bwd) + [
            pl.BlockSpec((LANES, GLA_QK), vec), pl.BlockSpec((LANES, GLA_QK), vec),
            pl.BlockSpec((1, GLA_QK), vec), pl.BlockSpec((1, GLA_QK), vec)],
        out_specs=[pl.BlockSpec((None, rb, GLA_V), fwd), pl.BlockSpec((None, rb, GLA_V), bwd)],
        out_shape=[jax.ShapeDtypeStruct((B, T, GLA_V), F32)] * 2,
        scratch_shapes=[pltpu.VMEM((2, GLA_HEADS, GLA_DV, GLA_DK), F32)],
        compiler_params=_params(("arbitrary", "arbitrary")),
        name="gla",
    )(qk, qk, vr, low, qk, qk, vr, low, w_up_f, w_up_b, b_f, b_b)


def _merge_kernel(att_ref, of_ref, ob_ref, r_ref, ga_ref, gg_ref, x_ref, gt_ref, sc_ref, sh_ref, ggla_ref, gffn_ref,
                  wa_ref, wg_ref, wo_ref, x1_ref, h2_ref):
    parts = []
    for h in range(GLA_HEADS):
        vs = slice(h * GLA_DV, (h + 1) * GLA_DV)
        o = of_ref[:, vs] + ob_ref[:, vs]
        ms = jnp.mean(o * o, axis=-1, keepdims=True)
        y = o * lax.rsqrt(ms + EPS) * ggla_ref[:, vs]
        parts.append((y * _silu(r_ref[:, vs].astype(F32))).astype(BF16))
    gl = jnp.concatenate(parts, axis=-1)
    a = jnp.dot(att_ref[...], wa_ref[...], preferred_element_type=F32)
    g = jnp.dot(gl, wg_ref[...], preferred_element_type=F32)
    gate_a = jax.nn.sigmoid(ga_ref[...].astype(F32))
    gate_g = jax.nn.sigmoid(gg_ref[...].astype(F32))
    merged = (gate_a * a + gate_g * g).astype(BF16)
    out = jnp.dot(merged, wo_ref[...], preferred_element_type=F32)
    x1 = x_ref[...] + gt_ref[...] * out
    x1_ref[...] = x1
    ms = jnp.mean(x1 * x1, axis=-1, keepdims=True)
    h2 = x1 * lax.rsqrt(ms + EPS) * gffn_ref[...]
    h2_ref[...] = (h2 * (1.0 + sc_ref[...]) + sh_ref[...]).astype(BF16)


def _merge(att, o_f, o_b, vr, gates, x, mod6, g_gla, g_ffn, w_br_att, w_br_gla, w_out):
    B, T, D = x.shape
    tm = min(TILES["merge"], T)
    row = lambda b, i: (b, i, 0)
    vec = lambda b, i: (0, 0)
    resident = lambda shape: pl.BlockSpec(shape, vec, pipeline_mode=pl.Buffered(1))
    return pl.pallas_call(
        _merge_kernel,
        grid=(B, T // tm),
        in_specs=[
            pl.BlockSpec((None, tm, ATT_W), row),
            pl.BlockSpec((None, tm, GLA_V), row),
            pl.BlockSpec((None, tm, GLA_V), row),
            pl.BlockSpec((None, tm, GLA_V), lambda b, i: (b, i, 1)),
            pl.BlockSpec((None, tm, D), lambda b, i: (b, i, 0)),
            pl.BlockSpec((None, tm, D), lambda b, i: (b, i, 1)),
            pl.BlockSpec((None, tm, D), row),
            pl.BlockSpec((None, 1, D), lambda b, i: (b * 6 + 2, 0, 0)),
            pl.BlockSpec((None, 1, D), lambda b, i: (b * 6 + 4, 0, 0)),
            pl.BlockSpec((None, 1, D), lambda b, i: (b * 6 + 3, 0, 0)),
            pl.BlockSpec((1, GLA_V), vec),
            pl.BlockSpec((1, D), vec),
            resident((ATT_W, D)),
            resident((GLA_V, D)),
            resident((D, D)),
        ],
        out_specs=[pl.BlockSpec((None, tm, D), row), pl.BlockSpec((None, tm, D), row)],
        out_shape=[jax.ShapeDtypeStruct((B, T, D), F32), jax.ShapeDtypeStruct((B, T, D), BF16)],
        compiler_params=_params(("arbitrary", "arbitrary")),
        name="merge",
    )(att, o_f, o_b, vr, gates, gates, x, mod6, mod6, mod6, g_gla, g_ffn, w_br_att, w_br_gla, w_out)


def _ffn_kernel(h_ref, hp_ref, hn_ref, wu_ref, cb_ref, wd_ref, x1_ref, gt_ref, gf_ref, perm_ref, y_ref, hext, u_scr,
                acc, *, tm):
    i = pl.program_id(1)
    j = pl.program_id(2)
    halo = BF16_ROWS
    sub = 8
    G = tm // sub
    lane_tiles = acc.shape[0]

    def lanes(c):
        return slice(c * LANES, (c + 1) * LANES)

    def gather_rows(ref, start, stride):
        return jnp.concatenate([ref[c, pl.ds(start, sub, stride=stride), :] for c in range(lane_tiles)], axis=1)

    @pl.when(j == 0)
    def _():
        has_prev = (i > 0).astype(F32)
        has_next = (i < pl.num_programs(1) - 1).astype(F32)
        before = hp_ref[...].astype(F32)[halo - 1:halo] * has_prev
        after = hn_ref[...].astype(F32)[0:1] * has_next
        edge = jnp.concatenate([before, after, jnp.zeros((halo - 2, before.shape[1]), F32)], axis=0)
        hext[tm:, :] = edge.astype(BF16)
        hext[:tm, :] = jnp.dot(perm_ref[...], h_ref[...], preferred_element_type=F32).astype(BF16)
        acc[...] = jnp.zeros_like(acc)

    tf = wd_ref.shape[0]
    row_id = lax.broadcasted_iota(jnp.int32, (sub, tf), 0)
    unit = min(FFN_UNIT_ROWS, tm)
    n_units = tm // unit

    def up(q):
        r0, r1 = q * unit, (q + 1) * unit + (halo if q == n_units - 1 else 0)
        lhs = hext[r0:r1, :]
        u_scr[r0:r1, :] = jnp.dot(lhs, wu_ref[...], preferred_element_type=F32)

    def conv(q, k):
        r0, r1 = q * unit, (q + 1) * unit
        cs = slice(k * tf, (k + 1) * tf)
        main = u_scr[r0:r1, cs]
        if q > 0:
            prev = u_scr[r0 - sub:r1 - sub, cs]
        else:
            before = u_scr[tm:tm + 1, cs]
            first = jnp.where(row_id == 0, before, pltpu.roll(u_scr[tm - sub:tm, cs], 1, axis=0))
            prev = jnp.concatenate([first, u_scr[0:r1 - sub, cs]], axis=0)
        if q < n_units - 1:
            nxt = u_scr[r0 + sub:r1 + sub, cs]
        else:
            after = u_scr[tm + 1:tm + 2, cs]
            last = jnp.where(row_id == sub - 1, after, pltpu.roll(u_scr[0:sub, cs], sub - 1, axis=0))
            nxt = jnp.concatenate([u_scr[r0 + sub:tm, cs], last], axis=0)
        return cb_ref[0:1, cs] * prev + cb_ref[1:2, cs] * main + cb_ref[2:3, cs] * nxt + cb_ref[3:4, cs]

    def gate_down(q):
        r0, r1 = q * unit, (q + 1) * unit
        act = (_silu(conv(q, 1)) * conv(q, 0)).astype(BF16)
        down = jnp.dot(act, wd_ref[...], preferred_element_type=F32)
        for c in range(lane_tiles):
            acc[c, r0:r1, :] += down[:, lanes(c)]

    issued = set()

    def ensure_up(q):
        if q not in issued:
            issued.add(q)
            up(q)

    ensure_up(n_units - 1)
    for q in range(n_units):
        for ahead in range(q, min(q + 3, n_units)):
            ensure_up(ahead)
        gate_down(q)

    @pl.when(j == pl.num_programs(2) - 1)
    def _():
        for g in range(G):
            rows = slice(g * sub, (g + 1) * sub)
            ffn = gather_rows(acc, (g % (G // sub)) * sub * sub + g // (G // sub), sub)
            x2 = x1_ref[rows, :] + gt_ref[...] * ffn
            ms = jnp.mean(x2 * x2, axis=-1, keepdims=True)
            y_ref[rows, :] = x2 * lax.rsqrt(ms + EPS) * gf_ref[...]


def _row_permutation(tm):
    pos = jnp.arange(tm)
    token = (pos % 8) * (tm // 8) + pos // 8
    return (token[:, None] == jnp.arange(tm)[None, :]).astype(BF16)


def _ffn(h2, x1, mod6, w_up, conv, w_down, g_final):
    B, T, D = x1.shape
    d_ff = w_down.shape[0]
    tm = min(TILES["ffn"], T)
    tf = TILES["ffn_cols"]
    nj = d_ff // tf
    halo = BF16_ROWS
    row = lambda b, i, j: (b, i, 0)
    nt = T // tm

    def h_tile(b, i, j):
        return _tile_ahead(b, i, jnp.where(j > 0, 1, 0), B, nt)

    def h_map(b, i, j):
        return (*h_tile(b, i, j), 0)

    def hp_map(b, i, j):
        bb, ii = h_tile(b, i, j)
        return (bb, jnp.maximum(ii * (tm // halo) - 1, 0), 0)

    def hn_map(b, i, j):
        bb, ii = h_tile(b, i, j)
        return (bb, jnp.minimum((ii + 1) * (tm // halo), T // halo - 1), 0)

    def x1_map(b, i, j):
        return (*_tile_ahead(b, i, jnp.where(j == 0, -1, 0), B, nt), 0)

    return pl.pallas_call(
        functools.partial(_ffn_kernel, tm=tm),
        grid=(B, T // tm, nj),
        in_specs=[
            pl.BlockSpec((None, tm, D), h_map),
            pl.BlockSpec((None, halo, D), hp_map),
            pl.BlockSpec((None, halo, D), hn_map),
            pl.BlockSpec((D, 2 * tf), lambda b, i, j: (0, j)),
            pl.BlockSpec((4, 2 * tf), lambda b, i, j: (0, j)),
            pl.BlockSpec((tf, D), lambda b, i, j: (j, 0)),
            pl.BlockSpec((None, tm, D), x1_map),
            pl.BlockSpec((None, 1, D), lambda b, i, j: (b * 6 + 5, 0, 0)),
            pl.BlockSpec((1, D), lambda b, i, j: (0, 0)),
            pl.BlockSpec((tm, tm), lambda b, i, j: (0, 0)),
        ],
        out_specs=pl.BlockSpec((None, tm, D), row),
        out_shape=jax.ShapeDtypeStruct((B, T, D), F32),
        scratch_shapes=[pltpu.VMEM((tm + halo, D), BF16),
                        pltpu.VMEM((tm + halo, 2 * tf), F32),
                        pltpu.VMEM((D // LANES, tm, LANES), F32)],
        compiler_params=_params(("arbitrary", "arbitrary", "arbitrary")),
        name="ffn",
    )(h2, h2, h2, w_up, conv, w_down, x1, mod6, g_final, _row_permutation(tm))


def _rope_tables(T):
    rows = T // GRID_W
    axis_dim = HEAD_DIM // 2
    row = jnp.repeat(jnp.arange(rows, dtype=F32), GRID_W)
    col = jnp.tile(jnp.arange(GRID_W, dtype=F32), rows)
    inv = ROPE_THETA ** (-jnp.arange(0, axis_dim, 2, dtype=F32) / axis_dim)
    ar = row[:, None] * inv
    ac = col[:, None] * inv
    ang = jnp.concatenate([ar, ar, ac, ac], axis=-1)
    return jnp.cos(ang), jnp.sin(ang)


def _rope_matrices():
    j = jnp.arange(IN_TN)
    src, dst = j[:, None], j[None, :]
    half = HEAD_DIM // 4
    lower = (dst % (2 * half)) < half
    rot = jnp.where(lower & (src == dst + half), -1.0, 0.0) + jnp.where(~lower & (src == dst - half), 1.0, 0.0)
    ones = jnp.where(src // HEAD_DIM == dst // HEAD_DIM, 1.0, 0.0)
    return ones.astype(BF16), rot.astype(BF16)


def _prep_weights(w_mod, b_mod, g_mix_norm, w_in, g_q, g_k, w_a_up_f, b_a_f, w_a_up_b, b_a_b, g_gla, w_br_att,
                  w_br_gla, w_out, g_ffn_norm, w_up, w_conv, b_conv, w_down, g_final):
    D = w_in.shape[1]
    low0 = ATT_COLS + QK_COLS + VR_COLS
    low1 = low0 + 2 * GLA_RANK
    wi = w_in[0]
    w_main = jnp.concatenate([wi[:, :low0], wi[:, low1:]], axis=1).astype(BF16)
    w_low = jnp.pad(wi[:, low0:low1], ((0, 0), (0, LANES - 2 * GLA_RANK))).astype(BF16)
    up_f = jnp.pad(w_a_up_f[0], ((0, LANES - GLA_RANK), (0, 0))).astype(BF16)
    up_b = jnp.pad(w_a_up_b[0], ((GLA_RANK, LANES - 2 * GLA_RANK), (0, 0))).astype(BF16)
    tf = TILES["ffn_cols"]

    def pair_blocks(w):
        rows, d_ff = w.shape[0], w.shape[1] // 2
        return w.reshape(rows, 2, d_ff // tf, tf).swapaxes(1, 2).reshape(rows, 2 * d_ff)

    return dict(
        w_mod=w_mod[0].astype(BF16), b_mod=b_mod[0], g_mix=g_mix_norm[0].reshape(1, D),
        w_main=w_main, w_low=w_low, g_q=g_q[0].reshape(1, HEAD_DIM) * (HEAD_DIM ** -0.5 * LOG2_E),
        g_k=g_k[0].reshape(1, HEAD_DIM),
        up_f=up_f, up_b=up_b, b_f=b_a_f[0].reshape(1, GLA_QK), b_b=b_a_b[0].reshape(1, GLA_QK),
        g_gla=g_gla[0].reshape(1, GLA_V), w_br_att=w_br_att[0].astype(BF16), w_br_gla=w_br_gla[0].astype(BF16),
        w_out=w_out[0].astype(BF16), g_ffn=g_ffn_norm[0].reshape(1, D), w_up=pair_blocks(w_up[0].astype(BF16)),
        conv=pair_blocks(jnp.concatenate([w_conv[0], b_conv[0][None, :]], axis=0)), w_down=w_down[0].astype(BF16),
        g_final=g_final.reshape(1, D))


def _trunk(x, mod, W):
    B, T, D = x.shape
    mod6 = mod.reshape(B * 6, 1, D)
    cos, sin = _rope_tables(T)
    ones_bd, rot_bd = _rope_matrices()
    qkv, qk, vr, gates, low = _inproj(x, mod6, W["g_mix"], W["w_main"], W["w_low"], W["g_q"], W["g_k"],
                                      cos, sin, ones_bd, rot_bd)
    att = _attention(qkv)
    o_f, o_b = _gla(qk, vr, low, W["up_f"], W["b_f"], W["up_b"], W["b_b"])
    x1, h2 = _merge(att, o_f, o_b, vr, gates, x, mod6, W["g_gla"], W["g_ffn"], W["w_br_att"], W["w_br_gla"],
                    W["w_out"])
    return _ffn(h2, x1, mod6, W["w_up"], W["conv"], W["w_down"], W["g_final"])


def kernel(x_prompt, x_sample, c_prompt, c_sample, w_mod, b_mod, g_mix_norm, w_in, g_q, g_k, w_a_up_f, b_a_f,
           w_a_up_b, b_a_b, g_gla, w_br_att, w_br_gla, w_out, g_ffn_norm, w_up, w_conv, b_conv, w_down, g_final):
    assert w_mod.shape[0] == 1, "single-layer trunk"
    W = _prep_weights(w_mod, b_mod, g_mix_norm, w_in, g_q, g_k, w_a_up_f, b_a_f, w_a_up_b, b_a_b, g_gla, w_br_att,
                      w_br_gla, w_out, g_ffn_norm, w_up, w_conv, b_conv, w_down, g_final)
    nb = x_prompt.shape[0]
    mod = _mod(jnp.concatenate([c_prompt, c_sample], axis=0), W["w_mod"], W["b_mod"])
    return _trunk(x_prompt, mod[:nb], W), _trunk(x_sample, mod[nb:], W)
```

```python
import functools

import jax
import jax.numpy as jnp
from jax import lax
from jax.experimental import pallas as pl
from jax.experimental.pallas import tpu as pltpu

F32 = jnp.float32
BF16 = jnp.bfloat16
EPS = 1e-6
LOG2_E = 1.4426950408889634

HEAD_DIM = 128
ATT_HEADS = 8
ATT_KV_HEADS = 2
ATT_GROUP = ATT_HEADS // ATT_KV_HEADS
ATT_W = ATT_HEADS * HEAD_DIM
ATT_KV_W = ATT_KV_HEADS * HEAD_DIM
GRID_W = 64
ROPE_THETA = 10000.0
GLA_HEADS = 4
GLA_DK = 128
GLA_DV = 256
GLA_QK = GLA_HEADS * GLA_DK
GLA_V = GLA_HEADS * GLA_DV
GLA_RANK = 16
GATE_NORM = 16.0
GLA_CHUNK = 64
LANES = 128
BF16_ROWS = 16
VMEM_LIMIT = 56 * 1024 * 1024

IN_TN = 512
GATE_TN = 1024
ATT_COLS = ATT_W + 2 * ATT_KV_W
QK_COLS = 2 * GLA_QK
VR_COLS = 2 * GLA_V
ATT_TILES = ATT_COLS // IN_TN
QK_TILES = QK_COLS // IN_TN
VR_TILES = VR_COLS // IN_TN
QK_END = ATT_TILES + QK_TILES
VR_END = QK_END + VR_TILES

TILES = dict(inproj=1024, attn=1024, gla_rows=256, merge=256, ffn=512, ffn_cols=512)
ATT_UNIT_ROWS = 128
FFN_UNIT_ROWS = 128

NT = (((1,), (1,)), ((), ()))
TN = (((0,), (0,)), ((), ()))


def _params(sem):
    return pltpu.CompilerParams(dimension_semantics=sem, vmem_limit_bytes=VMEM_LIMIT)


def _silu(v):
    return v * jax.nn.sigmoid(v)


def _tile_ahead(b, i, ahead, n_batch, n_tiles):
    lin = jnp.clip(b * n_tiles + i + ahead, 0, n_batch * n_tiles - 1)
    return lin // n_tiles, lin % n_tiles


def _mod_kernel(c_ref, w_ref, b_ref, o_ref):
    s = _silu(c_ref[...]).astype(BF16)
    o_ref[...] = jnp.dot(s, w_ref[...], preferred_element_type=F32) + b_ref[...]


def _mod(c, w_mod, b_mod):
    n, d = c.shape
    cols = w_mod.shape[1]
    tn = 2048
    return pl.pallas_call(
        _mod_kernel,
        grid=(cols // tn,),
        in_specs=[pl.BlockSpec((n, d), lambda j: (0, 0)),
                  pl.BlockSpec((d, tn), lambda j: (0, j)),
                  pl.BlockSpec((1, tn), lambda j: (0, j))],
        out_specs=pl.BlockSpec((n, tn), lambda j: (0, j)),
        out_shape=jax.ShapeDtypeStruct((n, cols), F32),
        compiler_params=_params(("arbitrary",)),
        name="mod",
    )(c, w_mod, b_mod.reshape(1, cols))


def _norm_rope(a, g, cos, sin, ones_bd, rot_bd):
    heads = a.shape[1] // HEAD_DIM
    ssq = jnp.dot((a * a).astype(BF16), ones_bd, preferred_element_type=F32)
    y = a * lax.rsqrt(ssq * (1.0 / HEAD_DIM) + EPS) * jnp.concatenate([g] * heads, axis=1)
    rot = jnp.dot(y.astype(BF16), rot_bd, preferred_element_type=F32)
    return y * jnp.concatenate([cos] * heads, axis=1) + rot * jnp.concatenate([sin] * heads, axis=1)


def _inproj_kernel(x_ref, sc_ref, sh_ref, g_ref, w_ref, wg_ref, wlow_ref, gq_ref, gk_ref, cos_ref, sin_ref, ones_ref,
                   rot_ref, att_ref, qk_ref, vr_ref, gate_ref, low_ref, h_scr):
    n = pl.program_id(2)

    @pl.when(n == 0)
    def _():
        x = x_ref[...]
        ms = jnp.mean(x * x, axis=-1, keepdims=True)
        h = x * lax.rsqrt(ms + EPS) * g_ref[...]
        hb = (h * (1.0 + sc_ref[...]) + sh_ref[...]).astype(BF16)
        h_scr[...] = hb
        low_ref[...] = jnp.dot(hb, wlow_ref[...], preferred_element_type=F32)

    def proj():
        return jnp.dot(h_scr[...], w_ref[...], preferred_element_type=F32)

    @pl.when(n < ATT_W // IN_TN)
    def _():
        att_ref[...] = _norm_rope(proj(), gq_ref[...], cos_ref[...], sin_ref[...], ones_ref[...],
                                  rot_ref[...]).astype(BF16)

    @pl.when(n == ATT_W // IN_TN)
    def _():
        acc = proj()
        att_ref[:, :ATT_KV_W] = _norm_rope(acc[:, :ATT_KV_W], gk_ref[...], cos_ref[...], sin_ref[...],
                                           ones_ref[:ATT_KV_W, :ATT_KV_W], rot_ref[:ATT_KV_W, :ATT_KV_W]).astype(BF16)
        att_ref[:, ATT_KV_W:] = acc[:, ATT_KV_W:].astype(BF16)

    @pl.when(n == ATT_TILES)
    def _():
        qk_ref[...] = proj() * (GLA_DK ** -0.5)

    @pl.when((n > ATT_TILES) & (n < QK_END))
    def _():
        qk_ref[...] = proj()

    @pl.when((n >= QK_END) & (n < VR_END))
    def _():
        vr_ref[...] = proj().astype(BF16)

    @pl.when(n >= VR_END)
    def _():
        gate_ref[...] = jnp.dot(h_scr[...], wg_ref[...], preferred_element_type=F32).astype(BF16)


def _inproj(x, mod6, g_mix, w_head, w_gate, w_low, g_q, g_k, cos, sin, ones_bd, rot_bd):
    B, T, D = x.shape
    tm = min(TILES["inproj"], T)
    gate_tiles = w_gate.shape[1] // GATE_TN
    n_tiles = VR_END + gate_tiles
    row = lambda b, i, n: (b, i, 0)
    vec = lambda b, i, n: (0, 0)
    tab = lambda b, i, n: (i, 0)

    def gate_w_map(b, i, n):
        early = jnp.where(n < VR_END // 2, gate_tiles - 1, 0)
        return (0, jnp.where(n >= VR_END, n - VR_END, early))

    def x_map(b, i, n):
        return (*_tile_ahead(b, i, jnp.where(n > 0, 1, 0), B, T // tm), 0)

    return pl.pallas_call(
        _inproj_kernel,
        grid=(B, T // tm, n_tiles),
        in_specs=[
            pl.BlockSpec((None, tm, D), x_map),
            pl.BlockSpec((None, 1, D), lambda b, i, n: (b * 6 + 1, 0, 0)),
            pl.BlockSpec((None, 1, D), lambda b, i, n: (b * 6 + 0, 0, 0)),
            pl.BlockSpec((1, D), vec),
            pl.BlockSpec((D, IN_TN), lambda b, i, n: (0, jnp.minimum(n, VR_END - 1))),
            pl.BlockSpec((D, GATE_TN), gate_w_map),
            pl.BlockSpec((D, LANES), vec),
            pl.BlockSpec((1, HEAD_DIM), vec),
            pl.BlockSpec((1, HEAD_DIM), vec),
            pl.BlockSpec((tm, HEAD_DIM), tab),
            pl.BlockSpec((tm, HEAD_DIM), tab),
            pl.BlockSpec((IN_TN, IN_TN), vec),
            pl.BlockSpec((IN_TN, IN_TN), vec),
        ],
        out_specs=[
            pl.BlockSpec((None, tm, IN_TN), lambda b, i, n: (b, i, jnp.minimum(n, ATT_TILES - 1))),
            pl.BlockSpec((None, tm, IN_TN), lambda b, i, n: (b, i, jnp.clip(n - ATT_TILES, 0, QK_TILES - 1))),
            pl.BlockSpec((None, tm, IN_TN), lambda b, i, n: (b, i, jnp.clip(n - QK_END, 0, VR_TILES - 1))),
            pl.BlockSpec((None, tm, GATE_TN), lambda b, i, n: (b, i, jnp.clip(n - VR_END, 0, gate_tiles - 1))),
            pl.BlockSpec((None, tm, LANES), row),
        ],
        out_shape=[
            jax.ShapeDtypeStruct((B, T, ATT_COLS), BF16),
            jax.ShapeDtypeStruct((B, T, QK_COLS), F32),
            jax.ShapeDtypeStruct((B, T, VR_COLS), BF16),
            jax.ShapeDtypeStruct((B, T, gate_tiles * GATE_TN), BF16),
            jax.ShapeDtypeStruct((B, T, LANES), F32),
        ],
        scratch_shapes=[pltpu.VMEM((tm, D), BF16)],
        compiler_params=_params(("arbitrary", "arbitrary", "arbitrary")),
        name="inproj",
    )(x, mod6, mod6, g_mix, w_head, w_gate, w_low, g_q, g_k, cos, sin, ones_bd, rot_bd)


def _attn_kernel(q_ref, k_ref, v_ref, o_ref, v_ext):
    @pl.when(pl.program_id(2) == 0)
    def _():
        v_ext[:, :HEAD_DIM] = v_ref[...]
        v_ext[:, HEAD_DIM:] = jnp.ones((v_ext.shape[0], HEAD_DIM), BF16)

    k = k_ref[...]
    v = v_ext[...]
    for g in range(ATT_GROUP):
        cs = slice(g * HEAD_DIM, (g + 1) * HEAD_DIM)
        for r0 in range(0, q_ref.shape[0], ATT_UNIT_ROWS):
            rows = slice(r0, r0 + ATT_UNIT_ROWS)
            s = lax.dot_general(q_ref[rows, cs], k, NT, preferred_element_type=F32)
            m = jnp.max(s, axis=-1, keepdims=True)
            p = jnp.exp2((s - m).astype(BF16))
            o = jnp.dot(p, v, preferred_element_type=F32)
            o_ref[rows, cs] = (o[:, :HEAD_DIM] / o[:, HEAD_DIM:]).astype(BF16)


def _attention(qkv):
    B, T, _ = qkv.shape
    tq = min(TILES["attn"], T)
    gw = ATT_GROUP * HEAD_DIM
    k0 = ATT_W // HEAD_DIM
    v0 = (ATT_W + ATT_KV_W) // HEAD_DIM
    return pl.pallas_call(
        _attn_kernel,
        grid=(B, ATT_KV_HEADS, T // tq),
        in_specs=[
            pl.BlockSpec((None, tq, gw), lambda b, kv, i: (b, i, kv)),
            pl.BlockSpec((None, T, HEAD_DIM), lambda b, kv, i: (b, 0, k0 + kv)),
            pl.BlockSpec((None, T, HEAD_DIM), lambda b, kv, i: (b, 0, v0 + kv)),
        ],
        out_specs=pl.BlockSpec((None, tq, gw), lambda b, kv, i: (b, i, kv)),
        out_shape=jax.ShapeDtypeStruct((B, T, ATT_W), BF16),
        scratch_shapes=[pltpu.VMEM((T, 2 * HEAD_DIM), BF16)],
        compiler_params=_params(("arbitrary", "arbitrary", "arbitrary")),
        name="attn",
    )(qkv, qkv, qkv)


def _log_sigmoid(z):
    return jnp.minimum(z, 0.0) - jnp.log(1.0 + jnp.exp(-jnp.abs(z)))


def _gla_scan(q_ref, k_ref, v_ref, low_ref, w_ref, b_ref, o_ref, s_scr, d, forward, sub):
    C = GLA_CHUNK
    R = sub * C
    t_i = lax.broadcasted_iota(jnp.int32, (R, R), 0)
    s_i = lax.broadcasted_iota(jnp.int32, (R, R), 1)
    same = (t_i // C) == (s_i // C)
    if forward:
        cum = same & (s_i <= t_i)
        keep_same = cum
        keep_cross = (s_i // C) < (t_i // C)
    else:
        cum = same & (s_i >= t_i)
        keep_same = same & (s_i > t_i)
        keep_cross = (s_i // C) > (t_i // C)
    cum = jnp.where(cum, 1.0, 0.0).astype(BF16)

    z = jnp.dot(low_ref[...].astype(BF16), w_ref[...], preferred_element_type=F32) + b_ref[...]
    yield
    la = _log_sigmoid(z) * (LOG2_E / GATE_NORM)
    hi = la.astype(BF16)
    r1 = la - hi.astype(F32)
    mid = r1.astype(BF16)
    lo = (r1 - mid.astype(F32)).astype(BF16)
    yield
    bc = (jnp.dot(cum, hi, preferred_element_type=F32) + jnp.dot(cum, mid, preferred_element_type=F32)
          + jnp.dot(cum, lo, preferred_element_type=F32))
    yield

    order = list(range(sub)) if forward else list(reversed(range(sub)))
    chunk = [bc[c * C:(c + 1) * C, :] for c in range(sub)]
    tot = [ch[C - 1:C, :] if forward else ch[0:1, :] for ch in chunk]
    before, run = {}, jnp.zeros_like(tot[0])
    for c in order:
        before[c] = run
        run = run + tot[c]
    whole = run
    anchor = before[order[sub // 2]]

    def per_chunk(fn):
        return jnp.concatenate([fn(c) for c in range(sub)], axis=0)

    e_q_same = jnp.exp2(bc)
    e_k_same = jnp.exp2(-bc)
    e_q_cross = jnp.exp2(per_chunk(lambda c: chunk[c] + (before[c] - anchor)))
    e_k_cross = jnp.exp2(per_chunk(lambda c: (tot[c] - chunk[c]) + (anchor - before[c] - tot[c])))
    e_q_state = jnp.exp2(per_chunk(lambda c: chunk[c] + before[c]))
    e_k_state = jnp.exp2(per_chunk(lambda c: (tot[c] - chunk[c]) + (whole - before[c] - tot[c])))
    e_whole = jnp.exp2(whole)
    yield

    for h in range(GLA_HEADS):
        ks = slice(h * GLA_DK, (h + 1) * GLA_DK)
        vs = slice(h * GLA_DV, (h + 1) * GLA_DV)
        q = q_ref[:, ks]
        k = k_ref[:, ks]
        v = v_ref[:, vs]
        a_same = lax.dot_general((q * e_q_same[:, ks]).astype(BF16), (k * e_k_same[:, ks]).astype(BF16), NT,
                                 preferred_element_type=F32)
        a_cross = lax.dot_general((q * e_q_cross[:, ks]).astype(BF16), (k * e_k_cross[:, ks]).astype(BF16), NT,
                                  preferred_element_type=F32)
        yield
        a = jnp.where(keep_same, a_same, jnp.where(keep_cross, a_cross, 0.0)).astype(BF16)
        st = s_scr[d, h]
        yield
        o_ref[:, vs] = (jnp.dot(a, v, preferred_element_type=F32)
                        + lax.dot_general((q * e_q_state[:, ks]).astype(BF16), st.astype(BF16), NT,
                                          preferred_element_type=F32))
        s_scr[d, h] = st * e_whole[:, ks] + lax.dot_general(v, (k * e_k_state[:, ks]).astype(BF16), TN,
                                                            preferred_element_type=F32)
        yield


def _alternate(stage_generators):
    live = list(stage_generators)
    while live:
        for g in list(live):
            try:
                next(g)
            except StopIteration:
                live.remove(g)


def _gla_kernel(qf, kf, vf, lf, qb, kb, vb, lb, wf, wb, bf_, bb_, of, ob, s_scr, *, sub):
    @pl.when(pl.program_id(1) == 0)
    def _():
        s_scr[...] = jnp.zeros_like(s_scr)

    scans = [_gla_scan(qf, kf, vf, lf, wf, bf_, of, s_scr, 0, True, sub),
             _gla_scan(qb, kb, vb, lb, wb, bb_, ob, s_scr, 1, False, sub)]
    _alternate(scans)


def _gla(qk, vr, low, w_up_f, b_f, w_up_b, b_b):
    B, T, _ = qk.shape
    rb = min(TILES["gla_rows"], T)
    nb = T // rb
    fwd = lambda b, c: (b, c, 0)
    bwd = lambda b, c: (b, nb - 1 - c, 0)
    vec = lambda b, c: (0, 0)

    def streams(row_map):
        r = lambda col: (lambda b, c: (*row_map(b, c)[:2], col))
        return [pl.BlockSpec((None, rb, GLA_QK), r(0)),
                pl.BlockSpec((None, rb, GLA_QK), r(1)),
                pl.BlockSpec((None, rb, GLA_V), r(0)),
                pl.BlockSpec((None, rb, LANES), r(0))]

    return pl.pallas_call(
        functools.partial(_gla_kernel, sub=rb // GLA_CHUNK),
        grid=(B, nb),
        in_specs=streams(fwd) + streams(bwd) + [
            pl.BlockSpec((LANES, GLA_QK), vec), pl.BlockSpec((LANES, GLA_QK), vec),
            pl.BlockSpec((1, GLA_QK), vec), pl.BlockSpec((1, GLA_QK), vec)],
        out_specs=[pl.BlockSpec((None, rb, GLA_V), fwd), pl.BlockSpec((None, rb, GLA_V), bwd)],
        out_shape=[jax.ShapeDtypeStruct((B, T, GLA_V), F32)] * 2,
        scratch_shapes=[pltpu.VMEM((2, GLA_HEADS, GLA_DV, GLA_DK), F32)],
        compiler_params=_params(("arbitrary", "arbitrary")),
        name="gla",
    )(qk, qk, vr, low, qk, qk, vr, low, w_up_f, w_up_b, b_f, b_b)


def _merge_kernel(att_ref, of_ref, ob_ref, r_ref, ga_ref, gg_ref, x_ref, gt_ref, sc_ref, sh_ref, ggla_ref, gffn_ref,
                  wa_ref, wg_ref, wo_ref, x1_ref, h2_ref):
    parts = []
    for h in range(GLA_HEADS):
        vs = slice(h * GLA_DV, (h + 1) * GLA_DV)
        o = of_ref[:, vs] + ob_ref[:, vs]
        ms = jnp.mean(o * o, axis=-1, keepdims=True)
        y = o * lax.rsqrt(ms + EPS) * ggla_ref[:, vs]
        parts.append((y * _silu(r_ref[:, vs].astype(F32))).astype(BF16))
    gl = jnp.concatenate(parts, axis=-1)
    a = jnp.dot(att_ref[...], wa_ref[...], preferred_element_type=F32)
    g = jnp.dot(gl, wg_ref[...], preferred_element_type=F32)
    gate_a = jax.nn.sigmoid(ga_ref[...].astype(F32))
    gate_g = jax.nn.sigmoid(gg_ref[...].astype(F32))
    merged = (gate_a * a + gate_g * g).astype(BF16)
    out = jnp.dot(merged, wo_ref[...], preferred_element_type=F32)
    x1 = x_ref[...] + gt_ref[...] * out
    x1_ref[...] = x1
    ms = jnp.mean(x1 * x1, axis=-1, keepdims=True)
    h2 = x1 * lax.rsqrt(ms + EPS) * gffn_ref[...]
    h2_ref[...] = (h2 * (1.0 + sc_ref[...]) + sh_ref[...]).astype(BF16)


def _merge(att, o_f, o_b, vr, gates, x, mod6, g_gla, g_ffn, w_br_att, w_br_gla, w_out):
    B, T, D = x.shape
    tm = min(TILES["merge"], T)
    row = lambda b, i: (b, i, 0)
    vec = lambda b, i: (0, 0)
    resident = lambda shape: pl.BlockSpec(shape, vec, pipeline_mode=pl.Buffered(1))
    return pl.pallas_call(
        _merge_kernel,
        grid=(B, T // tm),
        in_specs=[
            pl.BlockSpec((None, tm, ATT_W), row),
            pl.BlockSpec((None, tm, GLA_V), row),
            pl.BlockSpec((None, tm, GLA_V), row),
            pl.BlockSpec((None, tm, GLA_V), lambda b, i: (b, i, 1)),
            pl.BlockSpec((None, tm, D), lambda b, i: (b, i, 0)),
            pl.BlockSpec((None, tm, D), lambda b, i: (b, i, 1)),
            pl.BlockSpec((None, tm, D), row),
            pl.BlockSpec((None, 1, D), lambda b, i: (b * 6 + 2, 0, 0)),
            pl.BlockSpec((None, 1, D), lambda b, i: (b * 6 + 4, 0, 0)),
            pl.BlockSpec((None, 1, D), lambda b, i: (b * 6 + 3, 0, 0)),
            pl.BlockSpec((1, GLA_V), vec),
            pl.BlockSpec((1, D), vec),
            resident((ATT_W, D)),
            resident((GLA_V, D)),
            resident((D, D)),
        ],
        out_specs=[pl.BlockSpec((None, tm, D), row), pl.BlockSpec((None, tm, D), row)],
        out_shape=[jax.ShapeDtypeStruct((B, T, D), F32), jax.ShapeDtypeStruct((B, T, D), BF16)],
        compiler_params=_params(("arbitrary", "arbitrary")),
        name="merge",
    )(att, o_f, o_b, vr, gates, gates, x, mod6, mod6, mod6, g_gla, g_ffn, w_br_att, w_br_gla, w_out)


def _ffn_kernel(h_ref, hp_ref, hn_ref, wv_ref, wg_ref, cv_ref, cg_ref, bv_ref, bg_ref, wd_ref, x1_ref, gt_ref,
                gf_ref, perm_ref, y_ref, hext, u_scr, acc, *, tm):
    i = pl.program_id(1)
    j = pl.program_id(2)
    halo = BF16_ROWS
    sub = 8
    G = tm // sub
    lane_tiles = acc.shape[0]

    def lanes(c):
        return slice(c * LANES, (c + 1) * LANES)

    def gather_rows(ref, start, stride):
        return jnp.concatenate([ref[c, pl.ds(start, sub, stride=stride), :] for c in range(lane_tiles)], axis=1)

    @pl.when(j == 0)
    def _():
        has_prev = (i > 0).astype(F32)
        has_next = (i < pl.num_programs(1) - 1).astype(F32)
        before = hp_ref[...].astype(F32)[halo - 1:halo] * has_prev
        after = hn_ref[...].astype(F32)[0:1] * has_next
        edge = jnp.concatenate([before, after, jnp.zeros((halo - 2, before.shape[1]), F32)], axis=0)
        hext[tm:, :] = edge.astype(BF16)
        hext[:tm, :] = jnp.dot(perm_ref[...], h_ref[...], preferred_element_type=F32).astype(BF16)
        acc[...] = jnp.zeros_like(acc)

    row_id = lax.broadcasted_iota(jnp.int32, (sub, wv_ref.shape[1]), 0)
    unit = min(FFN_UNIT_ROWS, tm)
    n_units = tm // unit

    def up(q):
        r0, r1 = q * unit, (q + 1) * unit + (halo if q == n_units - 1 else 0)
        lhs = hext[r0:r1, :]
        u_scr[0, r0:r1, :] = jnp.dot(lhs, wv_ref[...], preferred_element_type=F32)
        u_scr[1, r0:r1, :] = jnp.dot(lhs, wg_ref[...], preferred_element_type=F32)

    def conv(q, k, c_ref, b_ref):
        r0, r1 = q * unit, (q + 1) * unit
        main = u_scr[k, r0:r1, :]
        if q > 0:
            prev = u_scr[k, r0 - sub:r1 - sub, :]
        else:
            before = u_scr[k, tm:tm + 1, :]
            first = jnp.where(row_id == 0, before, pltpu.roll(u_scr[k, tm - sub:tm, :], 1, axis=0))
            prev = jnp.concatenate([first, u_scr[k, 0:r1 - sub, :]], axis=0)
        if q < n_units - 1:
            nxt = u_scr[k, r0 + sub:r1 + sub, :]
        else:
            after = u_scr[k, tm + 1:tm + 2, :]
            last = jnp.where(row_id == sub - 1, after, pltpu.roll(u_scr[k, 0:sub, :], sub - 1, axis=0))
            nxt = jnp.concatenate([u_scr[k, r0 + sub:tm, :], last], axis=0)
        return c_ref[0:1, :] * prev + c_ref[1:2, :] * main + c_ref[2:3, :] * nxt + b_ref[...]

    def gate_down(q):
        r0, r1 = q * unit, (q + 1) * unit
        act = (_silu(conv(q, 1, cg_ref, bg_ref)) * conv(q, 0, cv_ref, bv_ref)).astype(BF16)
        down = jnp.dot(act, wd_ref[...], preferred_element_type=F32)
        for c in range(lane_tiles):
            acc[c, r0:r1, :] += down[:, lanes(c)]

    issued = set()

    def ensure_up(q):
        if q not in issued:
            issued.add(q)
            up(q)

    ensure_up(n_units - 1)
    for q in range(n_units):
        for ahead in range(q, min(q + 3, n_units)):
            ensure_up(ahead)
        gate_down(q)

    @pl.when(j == pl.num_programs(2) - 1)
    def _():
        for g in range(G):
            rows = slice(g * sub, (g + 1) * sub)
            ffn = gather_rows(acc, (g % (G // sub)) * sub * sub + g // (G // sub), sub)
            x2 = x1_ref[rows, :] + gt_ref[...] * ffn
            ms = jnp.mean(x2 * x2, axis=-1, keepdims=True)
            y_ref[rows, :] = x2 * lax.rsqrt(ms + EPS) * gf_ref[...]


def _row_permutation(tm):
    pos = jnp.arange(tm)
    token = (pos % 8) * (tm // 8) + pos // 8
    return (token[:, None] == jnp.arange(tm)[None, :]).astype(BF16)


def _ffn(h2, x1, mod6, w_up, w_conv, b_conv, w_down, g_final):
    B, T, D = x1.shape
    d_ff = w_down.shape[0]
    tm = min(TILES["ffn"], T)
    tf = TILES["ffn_cols"]
    nj = d_ff // tf
    halo = BF16_ROWS
    row = lambda b, i, j: (b, i, 0)
    nt = T // tm

    def h_tile(b, i, j):
        return _tile_ahead(b, i, jnp.where(j > 0, 1, 0), B, nt)

    def h_map(b, i, j):
        return (*h_tile(b, i, j), 0)

    def hp_map(b, i, j):
        bb, ii = h_tile(b, i, j)
        return (bb, jnp.maximum(ii * (tm // halo) - 1, 0), 0)

    def hn_map(b, i, j):
        bb, ii = h_tile(b, i, j)
        return (bb, jnp.minimum((ii + 1) * (tm // halo), T // halo - 1), 0)

    def x1_map(b, i, j):
        return (*_tile_ahead(b, i, jnp.where(j == 0, -1, 0), B, nt), 0)

    return pl.pallas_call(
        functools.partial(_ffn_kernel, tm=tm),
        grid=(B, T // tm, nj),
        in_specs=[
            pl.BlockSpec((None, tm, D), h_map),
            pl.BlockSpec((None, halo, D), hp_map),
            pl.BlockSpec((None, halo, D), hn_map),
            pl.BlockSpec((D, tf), lambda b, i, j: (0, j)),
            pl.BlockSpec((D, tf), lambda b, i, j: (0, nj + j)),
            pl.BlockSpec((3, tf), lambda b, i, j: (0, j)),
            pl.BlockSpec((3, tf), lambda b, i, j: (0, nj + j)),
            pl.BlockSpec((1, tf), lambda b, i, j: (0, j)),
            pl.BlockSpec((1, tf), lambda b, i, j: (0, nj + j)),
            pl.BlockSpec((tf, D), lambda b, i, j: (j, 0)),
            pl.BlockSpec((None, tm, D), x1_map),
            pl.BlockSpec((None, 1, D), lambda b, i, j: (b * 6 + 5, 0, 0)),
            pl.BlockSpec((1, D), lambda b, i, j: (0, 0)),
            pl.BlockSpec((tm, tm), lambda b, i, j: (0, 0)),
        ],
        out_specs=pl.BlockSpec((None, tm, D), row),
        out_shape=jax.ShapeDtypeStruct((B, T, D), F32),
        scratch_shapes=[pltpu.VMEM((tm + halo, D), BF16),
                        pltpu.VMEM((2, tm + halo, tf), F32),
                        pltpu.VMEM((D // LANES, tm, LANES), F32)],
        compiler_params=_params(("arbitrary", "arbitrary", "arbitrary")),
        name="ffn",
    )(h2, h2, h2, w_up, w_up, w_conv, w_conv, b_conv, b_conv, w_down, x1, mod6, g_final, _row_permutation(tm))


def _rope_tables(T):
    rows = T // GRID_W
    axis_dim = HEAD_DIM // 2
    row = jnp.repeat(jnp.arange(rows, dtype=F32), GRID_W)
    col = jnp.tile(jnp.arange(GRID_W, dtype=F32), rows)
    inv = ROPE_THETA ** (-jnp.arange(0, axis_dim, 2, dtype=F32) / axis_dim)
    ar = row[:, None] * inv
    ac = col[:, None] * inv
    ang = jnp.concatenate([ar, ar, ac, ac], axis=-1)
    return jnp.cos(ang), jnp.sin(ang)


def _rope_matrices():
    j = jnp.arange(IN_TN)
    src, dst = j[:, None], j[None, :]
    half = HEAD_DIM // 4
    lower = (dst % (2 * half)) < half
    rot = jnp.where(lower & (src == dst + half), -1.0, 0.0) + jnp.where(~lower & (src == dst - half), 1.0, 0.0)
    ones = jnp.where(src // HEAD_DIM == dst // HEAD_DIM, 1.0, 0.0)
    return ones.astype(BF16), rot.astype(BF16)


def _prep_weights(w_mod, b_mod, g_mix_norm, w_in, g_q, g_k, w_a_up_f, b_a_f, w_a_up_b, b_a_b, g_gla, w_br_att,
                  w_br_gla, w_out, g_ffn_norm, w_up, w_conv, b_conv, w_down, g_final):
    D = w_in.shape[1]
    low0 = ATT_COLS + QK_COLS + VR_COLS
    low1 = low0 + 2 * GLA_RANK
    wi = w_in[0]
    w_head = wi[:, :low0].astype(BF16)
    w_gate = wi[:, low1:].astype(BF16)
    w_low = jnp.pad(wi[:, low0:low1], ((0, 0), (0, LANES - 2 * GLA_RANK))).astype(BF16)
    up_f = jnp.pad(w_a_up_f[0], ((0, LANES - GLA_RANK), (0, 0))).astype(BF16)
    up_b = jnp.pad(w_a_up_b[0], ((GLA_RANK, LANES - 2 * GLA_RANK), (0, 0))).astype(BF16)
    return dict(
        w_mod=w_mod[0].astype(BF16), b_mod=b_mod[0], g_mix=g_mix_norm[0].reshape(1, D),
        w_head=w_head, w_gate=w_gate, w_low=w_low, g_q=g_q[0].reshape(1, HEAD_DIM) * (HEAD_DIM ** -0.5 * LOG2_E),
        g_k=g_k[0].reshape(1, HEAD_DIM),
        up_f=up_f, up_b=up_b, b_f=b_a_f[0].reshape(1, GLA_QK), b_b=b_a_b[0].reshape(1, GLA_QK),
        g_gla=g_gla[0].reshape(1, GLA_V), w_br_att=w_br_att[0].astype(BF16), w_br_gla=w_br_gla[0].astype(BF16),
        w_out=w_out[0].astype(BF16), g_ffn=g_ffn_norm[0].reshape(1, D), w_up=w_up[0].astype(BF16),
        w_conv=w_conv[0], b_conv=b_conv[0].reshape(1, -1), w_down=w_down[0].astype(BF16),
        g_final=g_final.reshape(1, D))


def _trunk(x, mod, W):
    B, T, D = x.shape
    mod6 = mod.reshape(B * 6, 1, D)
    cos, sin = _rope_tables(T)
    ones_bd, rot_bd = _rope_matrices()
    qkv, qk, vr, gates, low = _inproj(x, mod6, W["g_mix"], W["w_head"], W["w_gate"], W["w_low"], W["g_q"], W["g_k"],
                                      cos, sin, ones_bd, rot_bd)
    att = _attention(qkv)
    o_f, o_b = _gla(qk, vr, low, W["up_f"], W["b_f"], W["up_b"], W["b_b"])
    x1, h2 = _merge(att, o_f, o_b, vr, gates, x, mod6, W["g_gla"], W["g_ffn"], W["w_br_att"], W["w_br_gla"],
                    W["w_out"])
    return _ffn(h2, x1, mod6, W["w_up"], W["w_conv"], W["b_conv"], W["w_down"], W["g_final"])


def kernel(x_prompt, x_sample, c_prompt, c_sample, w_mod, b_mod, g_mix_norm, w_in, g_q, g_k, w_a_up_f, b_a_f,
           w_a_up_b, b_a_b, g_gla, w_br_att, w_br_gla, w_out, g_ffn_norm, w_up, w_conv, b_conv, w_down, g_final):
    assert w_mod.shape[0] == 1, "single-layer trunk"
    W = _prep_weights(w_mod, b_mod, g_mix_norm, w_in, g_q, g_k, w_a_up_f, b_a_f, w_a_up_b, b_a_b, g_gla, w_br_att,
                      w_br_gla, w_out, g_ffn_norm, w_up, w_conv, b_conv, w_down, g_final)
    nb = x_prompt.shape[0]
    mod = _mod(jnp.concatenate([c_prompt, c_sample], axis=0), W["w_mod"], W["b_mod"])
    return _trunk(x_prompt, mod[:nb], W), _trunk(x_sample, mod[nb:], W)
```

```python
import functools

import jax
import jax.numpy as jnp
from jax import lax
from jax.experimental import pallas as pl
from jax.experimental.pallas import tpu as pltpu

F32 = jnp.float32
BF16 = jnp.bfloat16
EPS = 1e-6
LOG2_E = 1.4426950408889634

HEAD_DIM = 128
ATT_HEADS = 8
ATT_KV_HEADS = 2
ATT_GROUP = ATT_HEADS // ATT_KV_HEADS
ATT_W = ATT_HEADS * HEAD_DIM
ATT_KV_W = ATT_KV_HEADS * HEAD_DIM
GRID_W = 64
ROPE_THETA = 10000.0
GLA_HEADS = 4
GLA_DK = 128
GLA_DV = 256
GLA_QK = GLA_HEADS * GLA_DK
GLA_V = GLA_HEADS * GLA_DV
GLA_RANK = 16
GATE_NORM = 16.0
GLA_CHUNK = 64
LANES = 128
BF16_ROWS = 16
VMEM_LIMIT = 56 * 1024 * 1024

IN_TN = 512
GATE_TN = 1024
ATT_COLS = ATT_W + 2 * ATT_KV_W
QK_COLS = 2 * GLA_QK
VR_COLS = 2 * GLA_V
ATT_TILES = ATT_COLS // IN_TN
QK_TILES = QK_COLS // IN_TN
VR_TILES = VR_COLS // IN_TN
QK_END = ATT_TILES + QK_TILES
VR_END = QK_END + VR_TILES

TILES = dict(inproj=1024, attn=1024, gla_rows=256, merge=256, ffn=512, ffn_cols=512)
ATT_UNIT_ROWS = 128
FFN_UNIT_ROWS = 128

NT = (((1,), (1,)), ((), ()))
TN = (((0,), (0,)), ((), ()))


def _params(sem):
    return pltpu.CompilerParams(dimension_semantics=sem, vmem_limit_bytes=VMEM_LIMIT)


def _silu(v):
    return v * jax.nn.sigmoid(v)


def _tile_ahead(b, i, ahead, n_batch, n_tiles):
    lin = jnp.clip(b * n_tiles + i + ahead, 0, n_batch * n_tiles - 1)
    return lin // n_tiles, lin % n_tiles


def _mod_kernel(c_ref, w_ref, b_ref, o_ref):
    s = _silu(c_ref[...]).astype(BF16)
    o_ref[...] = jnp.dot(s, w_ref[...], preferred_element_type=F32) + b_ref[...]


def _mod(c, w_mod, b_mod):
    n, d = c.shape
    cols = w_mod.shape[1]
    tn = 2048
    return pl.pallas_call(
        _mod_kernel,
        grid=(cols // tn,),
        in_specs=[pl.BlockSpec((n, d), lambda j: (0, 0)),
                  pl.BlockSpec((d, tn), lambda j: (0, j)),
                  pl.BlockSpec((1, tn), lambda j: (0, j))],
        out_specs=pl.BlockSpec((n, tn), lambda j: (0, j)),
        out_shape=jax.ShapeDtypeStruct((n, cols), F32),
        compiler_params=_params(("arbitrary",)),
        name="mod",
    )(c, w_mod, b_mod.reshape(1, cols))


def _norm_rope(a, g, cos, sin, ones_bd, rot_bd):
    heads = a.shape[1] // HEAD_DIM
    ssq = jnp.dot((a * a).astype(BF16), ones_bd, preferred_element_type=F32)
    y = a * lax.rsqrt(ssq * (1.0 / HEAD_DIM) + EPS) * jnp.concatenate([g] * heads, axis=1)
    rot = jnp.dot(y.astype(BF16), rot_bd, preferred_element_type=F32)
    return y * jnp.concatenate([cos] * heads, axis=1) + rot * jnp.concatenate([sin] * heads, axis=1)


def _inproj_kernel(x_ref, sc_ref, sh_ref, g_ref, w_ref, wg_ref, wlow_ref, gq_ref, gk_ref, cos_ref, sin_ref, ones_ref,
                   rot_ref, att_ref, qk_ref, vr_ref, gate_ref, low_ref, h_scr):
    n = pl.program_id(2)

    @pl.when(n == 0)
    def _():
        x = x_ref[...]
        ms = jnp.mean(x * x, axis=-1, keepdims=True)
        h = x * lax.rsqrt(ms + EPS) * g_ref[...]
        hb = (h * (1.0 + sc_ref[...]) + sh_ref[...]).astype(BF16)
        h_scr[...] = hb
        low_ref[...] = jnp.dot(hb, wlow_ref[...], preferred_element_type=F32)

    def proj():
        return jnp.dot(h_scr[...], w_ref[...], preferred_element_type=F32)

    @pl.when(n < ATT_W // IN_TN)
    def _():
        att_ref[...] = _norm_rope(proj(), gq_ref[...], cos_ref[...], sin_ref[...], ones_ref[...],
                                  rot_ref[...]).astype(BF16)

    @pl.when(n == ATT_W // IN_TN)
    def _():
        acc = proj()
        att_ref[:, :ATT_KV_W] = _norm_rope(acc[:, :ATT_KV_W], gk_ref[...], cos_ref[...], sin_ref[...],
                                           ones_ref[:ATT_KV_W, :ATT_KV_W], rot_ref[:ATT_KV_W, :ATT_KV_W]).astype(BF16)
        att_ref[:, ATT_KV_W:] = acc[:, ATT_KV_W:].astype(BF16)

    @pl.when(n == ATT_TILES)
    def _():
        qk_ref[...] = proj() * (GLA_DK ** -0.5)

    @pl.when((n > ATT_TILES) & (n < QK_END))
    def _():
        qk_ref[...] = proj()

    @pl.when((n >= QK_END) & (n < VR_END))
    def _():
        vr_ref[...] = proj().astype(BF16)

    @pl.when(n >= VR_END)
    def _():
        gate_ref[...] = jnp.dot(h_scr[...], wg_ref[...], preferred_element_type=F32).astype(BF16)


def _inproj(x, mod6, g_mix, w_head, w_gate, w_low, g_q, g_k, cos, sin, ones_bd, rot_bd):
    B, T, D = x.shape
    tm = min(TILES["inproj"], T)
    gate_tiles = w_gate.shape[1] // GATE_TN
    n_tiles = VR_END + gate_tiles
    row = lambda b, i, n: (b, i, 0)
    vec = lambda b, i, n: (0, 0)
    tab = lambda b, i, n: (i, 0)

    def gate_w_map(b, i, n):
        early = jnp.where(n < VR_END // 2, gate_tiles - 1, 0)
        return (0, jnp.where(n >= VR_END, n - VR_END, early))

    def x_map(b, i, n):
        return (*_tile_ahead(b, i, jnp.where(n > 0, 1, 0), B, T // tm), 0)

    return pl.pallas_call(
        _inproj_kernel,
        grid=(B, T // tm, n_tiles),
        in_specs=[
            pl.BlockSpec((None, tm, D), x_map),
            pl.BlockSpec((None, 1, D), lambda b, i, n: (b * 6 + 1, 0, 0)),
            pl.BlockSpec((None, 1, D), lambda b, i, n: (b * 6 + 0, 0, 0)),
            pl.BlockSpec((1, D), vec),
            pl.BlockSpec((D, IN_TN), lambda b, i, n: (0, jnp.minimum(n, VR_END - 1))),
            pl.BlockSpec((D, GATE_TN), gate_w_map),
            pl.BlockSpec((D, LANES), vec),
            pl.BlockSpec((1, HEAD_DIM), vec),
            pl.BlockSpec((1, HEAD_DIM), vec),
            pl.BlockSpec((tm, HEAD_DIM), tab),
            pl.BlockSpec((tm, HEAD_DIM), tab),
            pl.BlockSpec((IN_TN, IN_TN), vec),
            pl.BlockSpec((IN_TN, IN_TN), vec),
        ],
        out_specs=[
            pl.BlockSpec((None, tm, IN_TN), lambda b, i, n: (b, i, jnp.minimum(n, ATT_TILES - 1))),
            pl.BlockSpec((None, tm, IN_TN), lambda b, i, n: (b, i, jnp.clip(n - ATT_TILES, 0, QK_TILES - 1))),
            pl.BlockSpec((None, tm, IN_TN), lambda b, i, n: (b, i, jnp.clip(n - QK_END, 0, VR_TILES - 1))),
            pl.BlockSpec((None, tm, GATE_TN), lambda b, i, n: (b, i, jnp.clip(n - VR_END, 0, gate_tiles - 1))),
            pl.BlockSpec((None, tm, LANES), row),
        ],
        out_shape=[
            jax.ShapeDtypeStruct((B, T, ATT_COLS), BF16),
            jax.ShapeDtypeStruct((B, T, QK_COLS), F32),
            jax.ShapeDtypeStruct((B, T, VR_COLS), BF16),
            jax.ShapeDtypeStruct((B, T, gate_tiles * GATE_TN), BF16),
            jax.ShapeDtypeStruct((B, T, LANES), F32),
        ],
        scratch_shapes=[pltpu.VMEM((tm, D), BF16)],
        compiler_params=_params(("arbitrary", "arbitrary", "arbitrary")),
        name="inproj",
    )(x, mod6, mod6, g_mix, w_head, w_gate, w_low, g_q, g_k, cos, sin, ones_bd, rot_bd)


def _attn_kernel(q_ref, k_ref, v_ref, o_ref, v_ext):
    @pl.when(pl.program_id(2) == 0)
    def _():
        v_ext[:, :HEAD_DIM] = v_ref[...]
        v_ext[:, HEAD_DIM:] = jnp.ones((v_ext.shape[0], HEAD_DIM), BF16)

    k = k_ref[...]
    v = v_ext[...]
    for g in range(ATT_GROUP):
        cs = slice(g * HEAD_DIM, (g + 1) * HEAD_DIM)
        for r0 in range(0, q_ref.shape[0], ATT_UNIT_ROWS):
            rows = slice(r0, r0 + ATT_UNIT_ROWS)
            s = lax.dot_general(q_ref[rows, cs], k, NT, preferred_element_type=F32)
            m = jnp.max(s, axis=-1, keepdims=True)
            p = jnp.exp2((s - m).astype(BF16))
            o = jnp.dot(p, v, preferred_element_type=F32)
            o_ref[rows, cs] = (o[:, :HEAD_DIM] / o[:, HEAD_DIM:]).astype(BF16)


def _attention(qkv):
    B, T, _ = qkv.shape
    tq = min(TILES["attn"], T)
    gw = ATT_GROUP * HEAD_DIM
    k0 = ATT_W // HEAD_DIM
    v0 = (ATT_W + ATT_KV_W) // HEAD_DIM
    return pl.pallas_call(
        _attn_kernel,
        grid=(B, ATT_KV_HEADS, T // tq),
        in_specs=[
            pl.BlockSpec((None, tq, gw), lambda b, kv, i: (b, i, kv)),
            pl.BlockSpec((None, T, HEAD_DIM), lambda b, kv, i: (b, 0, k0 + kv)),
            pl.BlockSpec((None, T, HEAD_DIM), lambda b, kv, i: (b, 0, v0 + kv)),
        ],
        out_specs=pl.BlockSpec((None, tq, gw), lambda b, kv, i: (b, i, kv)),
        out_shape=jax.ShapeDtypeStruct((B, T, ATT_W), BF16),
        scratch_shapes=[pltpu.VMEM((T, 2 * HEAD_DIM), BF16)],
        compiler_params=_params(("arbitrary", "arbitrary", "arbitrary")),
        name="attn",
    )(qkv, qkv, qkv)


def _log_sigmoid(z):
    return jnp.minimum(z, 0.0) - jnp.log(1.0 + jnp.exp(-jnp.abs(z)))


def _gla_scan(q_ref, k_ref, v_ref, low_ref, w_ref, b_ref, o_ref, s_scr, d, forward, sub):
    C = GLA_CHUNK
    R = sub * C
    t_i = lax.broadcasted_iota(jnp.int32, (R, R), 0)
    s_i = lax.broadcasted_iota(jnp.int32, (R, R), 1)
    same = (t_i // C) == (s_i // C)
    if forward:
        cum = same & (s_i <= t_i)
        keep_same = cum
        keep_cross = (s_i // C) < (t_i // C)
    else:
        cum = same & (s_i >= t_i)
        keep_same = same & (s_i > t_i)
        keep_cross = (s_i // C) > (t_i // C)
    cum = jnp.where(cum, 1.0, 0.0).astype(BF16)

    z = jnp.dot(low_ref[...].astype(BF16), w_ref[...], preferred_element_type=F32) + b_ref[...]
    yield
    la = _log_sigmoid(z) * (LOG2_E / GATE_NORM)
    hi = la.astype(BF16)
    r1 = la - hi.astype(F32)
    mid = r1.astype(BF16)
    lo = (r1 - mid.astype(F32)).astype(BF16)
    yield
    bc = (jnp.dot(cum, hi, preferred_element_type=F32) + jnp.dot(cum, mid, preferred_element_type=F32)
          + jnp.dot(cum, lo, preferred_element_type=F32))
    yield

    order = list(range(sub)) if forward else list(reversed(range(sub)))
    chunk = [bc[c * C:(c + 1) * C, :] for c in range(sub)]
    tot = [ch[C - 1:C, :] if forward else ch[0:1, :] for ch in chunk]
    before, run = {}, jnp.zeros_like(tot[0])
    for c in order:
        before[c] = run
        run = run + tot[c]
    whole = run
    anchor = before[order[sub // 2]]

    def per_chunk(fn):
        return jnp.concatenate([fn(c) for c in range(sub)], axis=0)

    e_q_same = jnp.exp2(bc)
    e_k_same = jnp.exp2(-bc)
    e_q_cross = jnp.exp2(per_chunk(lambda c: chunk[c] + (before[c] - anchor)))
    e_k_cross = jnp.exp2(per_chunk(lambda c: (tot[c] - chunk[c]) + (anchor - before[c] - tot[c])))
    e_q_state = jnp.exp2(per_chunk(lambda c: chunk[c] + before[c]))
    e_k_state = jnp.exp2(per_chunk(lambda c: (tot[c] - chunk[c]) + (whole - before[c] - tot[c])))
    e_whole = jnp.exp2(whole)
    yield

    for h in range(GLA_HEADS):
        ks = slice(h * GLA_DK, (h + 1) * GLA_DK)
        vs = slice(h * GLA_DV, (h + 1) * GLA_DV)
        q = q_ref[:, ks]
        k = k_ref[:, ks]
        v = v_ref[:, vs]
        a_same = lax.dot_general((q * e_q_same[:, ks]).astype(BF16), (k * e_k_same[:, ks]).astype(BF16), NT,
                                 preferred_element_type=F32)
        a_cross = lax.dot_general((q * e_q_cross[:, ks]).astype(BF16), (k * e_k_cross[:, ks]).astype(BF16), NT,
                                  preferred_element_type=F32)
        yield
        a = jnp.where(keep_same, a_same, jnp.where(keep_cross, a_cross, 0.0)).astype(BF16)
        st = s_scr[d, h]
        yield
        o_ref[:, vs] = (jnp.dot(a, v, preferred_element_type=F32)
                        + lax.dot_general((q * e_q_state[:, ks]).astype(BF16), st.astype(BF16), NT,
                                          preferred_element_type=F32))
        s_scr[d, h] = st * e_whole[:, ks] + lax.dot_general(v, (k * e_k_state[:, ks]).astype(BF16), TN,
                                                            preferred_element_type=F32)
        yield


def _alternate(stage_generators):
    live = list(stage_generators)
    while live:
        for g in list(live):
            try:
                next(g)
            except StopIteration:
                live.remove(g)


def _gla_kernel(qf, kf, vf, lf, qb, kb, vb, lb, wf, wb, bf_, bb_, of, ob, s_scr, *, sub):
    @pl.when(pl.program_id(1) == 0)
    def _():
        s_scr[...] = jnp.zeros_like(s_scr)

    scans = [_gla_scan(qf, kf, vf, lf, wf, bf_, of, s_scr, 0, True, sub),
             _gla_scan(qb, kb, vb, lb, wb, bb_, ob, s_scr, 1, False, sub)]
    _alternate(scans)


def _gla(qk, vr, low, w_up_f, b_f, w_up_b, b_b):
    B, T, _ = qk.shape
    rb = min(TILES["gla_rows"], T)
    nb = T // rb
    fwd = lambda b, c: (b, c, 0)
    bwd = lambda b, c: (b, nb - 1 - c, 0)
    vec = lambda b, c: (0, 0)

    def streams(row_map):
        r = lambda col: (lambda b, c: (*row_map(b, c)[:2], col))
        return [pl.BlockSpec((None, rb, GLA_QK), r(0)),
                pl.BlockSpec((None, rb, GLA_QK), r(1)),
                pl.BlockSpec((None, rb, GLA_V), r(0)),
                pl.BlockSpec((None, rb, LANES), r(0))]

    return pl.pallas_call(
        functools.partial(_gla_kernel, sub=rb // GLA_CHUNK),
        grid=(B, nb),
        in_specs=streams(fwd) + streams(bwd) + [
            pl.BlockSpec((LANES, GLA_QK), vec), pl.BlockSpec((LANES, GLA_QK), vec),
            pl.BlockSpec((1, GLA_QK), vec), pl.BlockSpec((1, GLA_QK), vec)],
        out_specs=[pl.BlockSpec((None, rb, GLA_V), fwd), pl.BlockSpec((None, rb, GLA_V), bwd)],
        out_shape=[jax.ShapeDtypeStruct((B, T, GLA_V), F32)] * 2,
        scratch_shapes=[pltpu.VMEM((2, GLA_HEADS, GLA_DV, GLA_DK), F32)],
        compiler_params=_params(("arbitrary", "arbitrary")),
        name="gla",
    )(qk, qk, vr, low, qk, qk, vr, low, w_up_f, w_up_b, b_f, b_b)


def _merge_kernel(att_ref, of_ref, ob_ref, r_ref, ga_ref, gg_ref, x_ref, gt_ref, sc_ref, sh_ref, ggla_ref, gffn_ref,
                  wa_ref, wg_ref, wo_ref, x1_ref, h2_ref):
    parts = []
    for h in range(GLA_HEADS):
        vs = slice(h * GLA_DV, (h + 1) * GLA_DV)
        o = of_ref[:, vs] + ob_ref[:, vs]
        ms = jnp.mean(o * o, axis=-1, keepdims=True)
        y = o * lax.rsqrt(ms + EPS) * ggla_ref[:, vs]
        parts.append((y * _silu(r_ref[:, vs].astype(F32))).astype(BF16))
    gl = jnp.concatenate(parts, axis=-1)
    a = jnp.dot(att_ref[...], wa_ref[...], preferred_element_type=F32)
    g = jnp.dot(gl, wg_ref[...], preferred_element_type=F32)
    gate_a = jax.nn.sigmoid(ga_ref[...].astype(F32))
    gate_g = jax.nn.sigmoid(gg_ref[...].astype(F32))
    merged = (gate_a * a + gate_g * g).astype(BF16)
    out = jnp.dot(merged, wo_ref[...], preferred_element_type=F32)
    x1 = x_ref[...] + gt_ref[...] * out
    x1_ref[...] = x1
    ms = jnp.mean(x1 * x1, axis=-1, keepdims=True)
    h2 = x1 * lax.rsqrt(ms + EPS) * gffn_ref[...]
    h2_ref[...] = (h2 * (1.0 + sc_ref[...]) + sh_ref[...]).astype(BF16)


def _merge(att, o_f, o_b, vr, gates, x, mod6, g_gla, g_ffn, w_br_att, w_br_gla, w_out):
    B, T, D = x.shape
    tm = min(TILES["merge"], T)
    row = lambda b, i: (b, i, 0)
    vec = lambda b, i: (0, 0)
    resident = lambda shape: pl.BlockSpec(shape, vec, pipeline_mode=pl.Buffered(1))
    return pl.pallas_call(
        _merge_kernel,
        grid=(B, T // tm),
        in_specs=[
            pl.BlockSpec((None, tm, ATT_W), row),
            pl.BlockSpec((None, tm, GLA_V), row),
            pl.BlockSpec((None, tm, GLA_V), row),
            pl.BlockSpec((None, tm, GLA_V), lambda b, i: (b, i, 1)),
            pl.BlockSpec((None, tm, D), lambda b, i: (b, i, 0)),
            pl.BlockSpec((None, tm, D), lambda b, i: (b, i, 1)),
            pl.BlockSpec((None, tm, D), row),
            pl.BlockSpec((None, 1, D), lambda b, i: (b * 6 + 2, 0, 0)),
            pl.BlockSpec((None, 1, D), lambda b, i: (b * 6 + 4, 0, 0)),
            pl.BlockSpec((None, 1, D), lambda b, i: (b * 6 + 3, 0, 0)),
            pl.BlockSpec((1, GLA_V), vec),
            pl.BlockSpec((1, D), vec),
            resident((ATT_W, D)),
            resident((GLA_V, D)),
            resident((D, D)),
        ],
        out_specs=[pl.BlockSpec((None, tm, D), row), pl.BlockSpec((None, tm, D), row)],
        out_shape=[jax.ShapeDtypeStruct((B, T, D), F32), jax.ShapeDtypeStruct((B, T, D), BF16)],
        compiler_params=_params(("arbitrary", "arbitrary")),
        name="merge",
    )(att, o_f, o_b, vr, gates, gates, x, mod6, mod6, mod6, g_gla, g_ffn, w_br_att, w_br_gla, w_out)


def _ffn_kernel(h_ref, hp_ref, hn_ref, wv_ref, wg_ref, cv_ref, cg_ref, bv_ref, bg_ref, wd_ref, x1_ref, gt_ref,
                gf_ref, perm_ref, y_ref, hext, u_scr, acc, *, tm):
    i = pl.program_id(1)
    j = pl.program_id(2)
    halo = BF16_ROWS
    sub = 8
    G = tm // sub
    lane_tiles = acc.shape[0]

    def lanes(c):
        return slice(c * LANES, (c + 1) * LANES)

    def gather_rows(ref, start, stride):
        return jnp.concatenate([ref[c, pl.ds(start, sub, stride=stride), :] for c in range(lane_tiles)], axis=1)

    @pl.when(j == 0)
    def _():
        has_prev = (i > 0).astype(F32)
        has_next = (i < pl.num_programs(1) - 1).astype(F32)
        before = hp_ref[...].astype(F32)[halo - 1:halo] * has_prev
        after = hn_ref[...].astype(F32)[0:1] * has_next
        edge = jnp.concatenate([before, after, jnp.zeros((halo - 2, before.shape[1]), F32)], axis=0)
        hext[tm:, :] = edge.astype(BF16)
        hext[:tm, :] = jnp.dot(perm_ref[...], h_ref[...], preferred_element_type=F32).astype(BF16)

    row_id = lax.broadcasted_iota(jnp.int32, (sub, wv_ref.shape[1]), 0)
    unit = min(FFN_UNIT_ROWS, tm)
    n_units = tm // unit

    def up(q):
        r0, r1 = q * unit, (q + 1) * unit + (halo if q == n_units - 1 else 0)
        lhs = hext[r0:r1, :]
        u_scr[0, r0:r1, :] = jnp.dot(lhs, wv_ref[...], preferred_element_type=F32)
        u_scr[1, r0:r1, :] = jnp.dot(lhs, wg_ref[...], preferred_element_type=F32)

    def conv(q, k, c_ref, b_ref):
        r0, r1 = q * unit, (q + 1) * unit
        main = u_scr[k, r0:r1, :]
        if q > 0:
            prev = u_scr[k, r0 - sub:r1 - sub, :]
        else:
            before = u_scr[k, tm:tm + 1, :]
            first = jnp.where(row_id == 0, before, pltpu.roll(u_scr[k, tm - sub:tm, :], 1, axis=0))
            prev = jnp.concatenate([first, u_scr[k, 0:r1 - sub, :]], axis=0)
        if q < n_units - 1:
            nxt = u_scr[k, r0 + sub:r1 + sub, :]
        else:
            after = u_scr[k, tm + 1:tm + 2, :]
            last = jnp.where(row_id == sub - 1, after, pltpu.roll(u_scr[k, 0:sub, :], sub - 1, axis=0))
            nxt = jnp.concatenate([u_scr[k, r0 + sub:tm, :], last], axis=0)
        return c_ref[0:1, :] * prev + c_ref[1:2, :] * main + c_ref[2:3, :] * nxt + b_ref[...]

    def gate_down(q, first):
        r0, r1 = q * unit, (q + 1) * unit
        act = (_silu(conv(q, 1, cg_ref, bg_ref)) * conv(q, 0, cv_ref, bv_ref)).astype(BF16)
        down = jnp.dot(act, wd_ref[...], preferred_element_type=F32)
        for c in range(lane_tiles):
            if first:
                acc[c, r0:r1, :] = down[:, lanes(c)]
            else:
                acc[c, r0:r1, :] += down[:, lanes(c)]

    def step(first):
        issued = set()

        def ensure_up(q):
            if q not in issued:
                issued.add(q)
                up(q)

        ensure_up(n_units - 1)
        for q in range(n_units):
            for ahead in range(q, min(q + 3, n_units)):
                ensure_up(ahead)
            gate_down(q, first)

    @pl.when(j == 0)
    def _():
        step(True)

    @pl.when(j > 0)
    def _():
        step(False)

    @pl.when(j == pl.num_programs(2) - 1)
    def _():
        for g in range(G):
            rows = slice(g * sub, (g + 1) * sub)
            ffn = gather_rows(acc, (g % (G // sub)) * sub * sub + g // (G // sub), sub)
            x2 = x1_ref[rows, :] + gt_ref[...] * ffn
            ms = jnp.mean(x2 * x2, axis=-1, keepdims=True)
            y_ref[rows, :] = x2 * lax.rsqrt(ms + EPS) * gf_ref[...]


def _row_permutation(tm):
    pos = jnp.arange(tm)
    token = (pos % 8) * (tm // 8) + pos // 8
    return (token[:, None] == jnp.arange(tm)[None, :]).astype(BF16)


def _ffn(h2, x1, mod6, w_up, w_conv, b_conv, w_down, g_final):
    B, T, D = x1.shape
    d_ff = w_down.shape[0]
    tm = min(TILES["ffn"], T)
    tf = TILES["ffn_cols"]
    nj = d_ff // tf
    halo = BF16_ROWS
    row = lambda b, i, j: (b, i, 0)
    nt = T // tm

    def h_tile(b, i, j):
        return _tile_ahead(b, i, jnp.where(j > 0, 1, 0), B, nt)

    def h_map(b, i, j):
        return (*h_tile(b, i, j), 0)

    def hp_map(b, i, j):
        bb, ii = h_tile(b, i, j)
        return (bb, jnp.maximum(ii * (tm // halo) - 1, 0), 0)

    def hn_map(b, i, j):
        bb, ii = h_tile(b, i, j)
        return (bb, jnp.minimum((ii + 1) * (tm // halo), T // halo - 1), 0)

    def x1_map(b, i, j):
        return (*_tile_ahead(b, i, jnp.where(j == 0, -1, 0), B, nt), 0)

    return pl.pallas_call(
        functools.partial(_ffn_kernel, tm=tm),
        grid=(B, T // tm, nj),
        in_specs=[
            pl.BlockSpec((None, tm, D), h_map),
            pl.BlockSpec((None, halo, D), hp_map),
            pl.BlockSpec((None, halo, D), hn_map),
            pl.BlockSpec((D, tf), lambda b, i, j: (0, j)),
            pl.BlockSpec((D, tf), lambda b, i, j: (0, nj + j)),
            pl.BlockSpec((3, tf), lambda b, i, j: (0, j)),
            pl.BlockSpec((3, tf), lambda b, i, j: (0, nj + j)),
            pl.BlockSpec((1, tf), lambda b, i, j: (0, j)),
            pl.BlockSpec((1, tf), lambda b, i, j: (0, nj + j)),
            pl.BlockSpec((tf, D), lambda b, i, j: (j, 0)),
            pl.BlockSpec((None, tm, D), x1_map),
            pl.BlockSpec((None, 1, D), lambda b, i, j: (b * 6 + 5, 0, 0)),
            pl.BlockSpec((1, D), lambda b, i, j: (0, 0)),
            pl.BlockSpec((tm, tm), lambda b, i, j: (0, 0)),
        ],
        out_specs=pl.BlockSpec((None, tm, D), row),
        out_shape=jax.ShapeDtypeStruct((B, T, D), F32),
        scratch_shapes=[pltpu.VMEM((tm + halo, D), BF16),
                        pltpu.VMEM((2, tm + halo, tf), F32),
                        pltpu.VMEM((D // LANES, tm, LANES), F32)],
        compiler_params=_params(("arbitrary", "arbitrary", "arbitrary")),
        name="ffn",
    )(h2, h2, h2, w_up, w_up, w_conv, w_conv, b_conv, b_conv, w_down, x1, mod6, g_final, _row_permutation(tm))


def _rope_tables(T):
    rows = T // GRID_W
    axis_dim = HEAD_DIM // 2
    row = jnp.repeat(jnp.arange(rows, dtype=F32), GRID_W)
    col = jnp.tile(jnp.arange(GRID_W, dtype=F32), rows)
    inv = ROPE_THETA ** (-jnp.arange(0, axis_dim, 2, dtype=F32) / axis_dim)
    ar = row[:, None] * inv
    ac = col[:, None] * inv
    ang = jnp.concatenate([ar, ar, ac, ac], axis=-1)
    return jnp.cos(ang), jnp.sin(ang)


def _rope_matrices():
    j = jnp.arange(IN_TN)
    src, dst = j[:, None], j[None, :]
    half = HEAD_DIM // 4
    lower = (dst % (2 * half)) < half
    rot = jnp.where(lower & (src == dst + half), -1.0, 0.0) + jnp.where(~lower & (src == dst - half), 1.0, 0.0)
    ones = jnp.where(src // HEAD_DIM == dst // HEAD_DIM, 1.0, 0.0)
    return ones.astype(BF16), rot.astype(BF16)


def _prep_weights(w_mod, b_mod, g_mix_norm, w_in, g_q, g_k, w_a_up_f, b_a_f, w_a_up_b, b_a_b, g_gla, w_br_att,
                  w_br_gla, w_out, g_ffn_norm, w_up, w_conv, b_conv, w_down, g_final):
    D = w_in.shape[1]
    low0 = ATT_COLS + QK_COLS + VR_COLS
    low1 = low0 + 2 * GLA_RANK
    wi = w_in[0]
    w_head = wi[:, :low0].astype(BF16)
    w_gate = wi[:, low1:].astype(BF16)
    w_low = jnp.pad(wi[:, low0:low1], ((0, 0), (0, LANES - 2 * GLA_RANK))).astype(BF16)
    up_f = jnp.pad(w_a_up_f[0], ((0, LANES - GLA_RANK), (0, 0))).astype(BF16)
    up_b = jnp.pad(w_a_up_b[0], ((GLA_RANK, LANES - 2 * GLA_RANK), (0, 0))).astype(BF16)
    return dict(
        w_mod=w_mod[0].astype(BF16), b_mod=b_mod[0], g_mix=g_mix_norm[0].reshape(1, D),
        w_head=w_head, w_gate=w_gate, w_low=w_low, g_q=g_q[0].reshape(1, HEAD_DIM) * (HEAD_DIM ** -0.5 * LOG2_E),
        g_k=g_k[0].reshape(1, HEAD_DIM),
        up_f=up_f, up_b=up_b, b_f=b_a_f[0].reshape(1, GLA_QK), b_b=b_a_b[0].reshape(1, GLA_QK),
        g_gla=g_gla[0].reshape(1, GLA_V), w_br_att=w_br_att[0].astype(BF16), w_br_gla=w_br_gla[0].astype(BF16),
        w_out=w_out[0].astype(BF16), g_ffn=g_ffn_norm[0].reshape(1, D), w_up=w_up[0].astype(BF16),
        w_conv=w_conv[0], b_conv=b_conv[0].reshape(1, -1), w_down=w_down[0].astype(BF16),
        g_final=g_final.reshape(1, D))


def _trunk(x, mod, W):
    B, T, D = x.shape
    mod6 = mod.reshape(B * 6, 1, D)
    cos, sin = _rope_tables(T)
    ones_bd, rot_bd = _rope_matrices()
    qkv, qk, vr, gates, low = _inproj(x, mod6, W["g_mix"], W["w_head"], W["w_gate"], W["w_low"], W["g_q"], W["g_k"],
                                      cos, sin, ones_bd, rot_bd)
    att = _attention(qkv)
    o_f, o_b = _gla(qk, vr, low, W["up_f"], W["b_f"], W["up_b"], W["b_b"])
    x1, h2 = _merge(att, o_f, o_b, vr, gates, x, mod6, W["g_gla"], W["g_ffn"], W["w_br_att"], W["w_br_gla"],
                    W["w_out"])
    return _ffn(h2, x1, mod6, W["w_up"], W["w_conv"], W["b_conv"], W["w_down"], W["g_final"])


def kernel(x_prompt, x_sample, c_prompt, c_sample, w_mod, b_mod, g_mix_norm, w_in, g_q, g_k, w_a_up_f, b_a_f,
           w_a_up_b, b_a_b, g_gla, w_br_att, w_br_gla, w_out, g_ffn_norm, w_up, w_conv, b_conv, w_down, g_final):
    assert w_mod.shape[0] == 1, "single-layer trunk"
    W = _prep_weights(w_mod, b_mod, g_mix_norm, w_in, g_q, g_k, w_a_up_f, b_a_f, w_a_up_b, b_a_b, g_gla, w_br_att,
                      w_br_gla, w_out, g_ffn_norm, w_up, w_conv, b_conv, w_down, g_final)
    nb = x_prompt.shape[0]
    mod = _mod(jnp.concatenate([c_prompt, c_sample], axis=0), W["w_mod"], W["b_mod"])
    return _trunk(x_prompt, mod[:nb], W), _trunk(x_sample, mod[nb:], W)
```
